```python
import math
import jax, jax.numpy as jnp
from jax import lax
import numpy as np

D_MODEL = 2048
BATCH = 2
SEQ = 4096
DEPTH = 4

HEAD_DIM = 64
MIX_WIDTH = D_MODEL
NSA_HEADS = MIX_WIDTH // 2 // HEAD_DIM
NSA_KV_HEADS = NSA_HEADS // 4
NSA_GROUP = NSA_HEADS // NSA_KV_HEADS
SWA_HEADS = (MIX_WIDTH - NSA_HEADS * HEAD_DIM) // HEAD_DIM
SWA_KV_HEADS = SWA_HEADS // 8
SWA_GROUP = SWA_HEADS // SWA_KV_HEADS
CMP_LEN = 32
CMP_STRIDE = 16
CMP_HIDDEN = 4 * HEAD_DIM
SEL_BLOCK = 64
N_SEL = 16
NSA_WINDOW = 512
SWA_WINDOW = 128
NUM_BUCKETS = 32
REL_MAX_DISTANCE = 1024
N_BIAS_HEADS = NSA_HEADS + SWA_HEADS
D_FF = -(-8 * D_MODEL // (3 * 256)) * 256
QBLOCK = 128
RMS_EPS = 1e-5
NEG_INF = -1e30
FORCE_SCORE = 1e6

SPLIT_SIZES = (
    NSA_HEADS * HEAD_DIM,
    NSA_KV_HEADS * HEAD_DIM, NSA_KV_HEADS * HEAD_DIM,
    NSA_KV_HEADS * HEAD_DIM, NSA_KV_HEADS * HEAD_DIM,
    NSA_KV_HEADS * HEAD_DIM, NSA_KV_HEADS * HEAD_DIM,
    NSA_HEADS * 3,
    SWA_HEADS * HEAD_DIM,
    SWA_KV_HEADS * HEAD_DIM, SWA_KV_HEADS * HEAD_DIM,
)
PROJ_WIDTH = sum(SPLIT_SIZES)

kernel_name = "hybrid_nsa_swa_sink_t5bias_swiglu"


def rms_norm(x, g):
    xf = x.astype(jnp.float32)
    y = xf * lax.rsqrt(jnp.mean(xf * xf, axis=-1, keepdims=True) + RMS_EPS)
    return (y * g.astype(jnp.float32)).astype(x.dtype)


def t5_bucket(dist):
    dist = jnp.maximum(dist, 0)
    max_exact = NUM_BUCKETS // 2
    d = jnp.maximum(dist, 1).astype(jnp.float32)
    ratio = jnp.log(d / max_exact) / math.log(REL_MAX_DISTANCE / max_exact)
    large = max_exact + (ratio * (NUM_BUCKETS - max_exact)).astype(jnp.int32)
    large = jnp.minimum(large, NUM_BUCKETS - 1)
    return jnp.where(dist < max_exact, dist, large)


def masked_softmax(s, mask):
    s = jnp.where(mask, s, NEG_INF)
    p = jax.nn.softmax(s, axis=-1)
    return jnp.where(mask, p, 0.0)


def split_columns(proj):
    parts, off = [], 0
    for size in SPLIT_SIZES:
        parts.append(proj[..., off:off + size])
        off += size
    return parts


def to_heads(t, n_heads):
    return t.reshape(t.shape[0], t.shape[1], n_heads, HEAD_DIM)


def compress(kv, pos_emb, w1, w2):
    b, t, hkv, dh = kv.shape
    n_cmp = (t - CMP_LEN) // CMP_STRIDE + 1
    idx = jnp.arange(n_cmp)[:, None] * CMP_STRIDE + jnp.arange(CMP_LEN)[None, :]
    blocks = kv[:, idx] + pos_emb[:, None, :]
    flat = blocks.transpose(0, 1, 3, 2, 4).reshape(b, n_cmp, hkv, CMP_LEN * dh)
    return jax.nn.gelu(flat @ w1) @ w2


def nsa_attention(q, k_c, v_c, k_s, v_s, k_w, v_w, gates, tbl,
                  pos_k, pos_v, ck1, ck2, cv1, cv2):
    b, t = q.shape[0], q.shape[1]
    hkv, grp, dh = NSA_KV_HEADS, NSA_GROUP, HEAD_DIM
    scale = dh ** -0.5
    kc = compress(k_c, pos_k, ck1, ck2)
    vc = compress(v_c, pos_v, cv1, cv2)
    n_cmp = kc.shape[1]
    n_blk = t // SEL_BLOCK
    n_sel = min(N_SEL, n_blk)
    c_end = jnp.arange(n_cmp) * CMP_STRIDE + CMP_LEN - 1
    ci = jnp.arange(n_cmp)[:, None] * CMP_STRIDE
    sj = jnp.arange(n_blk)[None, :] * SEL_BLOCK
    overlap = ((ci < sj + SEL_BLOCK) & (ci + CMP_LEN > sj)).astype(jnp.float32)
    ks_blocks = k_s.reshape(b, n_blk, SEL_BLOCK, hkv, dh).transpose(0, 3, 1, 2, 4)
    vs_blocks = v_s.reshape(b, n_blk, SEL_BLOCK, hkv, dh).transpose(0, 3, 1, 2, 4)
    kwp = jnp.pad(k_w, ((0, 0), (NSA_WINDOW, 0), (0, 0), (0, 0)))
    vwp = jnp.pad(v_w, ((0, 0), (NSA_WINDOW, 0), (0, 0), (0, 0)))
    tbl_g = tbl.reshape(NUM_BUCKETS, hkv, grp)
    tbl_h = tbl_g.transpose(1, 2, 0).astype(jnp.float32)
    b_i = jnp.arange(b)[:, None, None, None]
    h_i = jnp.arange(hkv)[None, :, None, None]
    blk_ids = jnp.arange(n_blk)

    def block(qb):
        t0 = qb * QBLOCK
        tq = t0 + jnp.arange(QBLOCK)
        qblk = lax.dynamic_slice_in_dim(q, t0, QBLOCK, 1).reshape(b, QBLOCK, hkv, grp, dh)
        gblk = lax.dynamic_slice_in_dim(gates, t0, QBLOCK, 1).reshape(b, QBLOCK, hkv, grp, 3)

        dist_c = tq[:, None] - c_end[None, :]
        s_c = jnp.einsum('bqhgd,bchd->bhgqc', qblk, kc).astype(jnp.float32) * scale
        s_c = s_c + tbl_g[t5_bucket(dist_c)].transpose(2, 3, 0, 1).astype(jnp.float32)
        p_c = masked_softmax(s_c, dist_c >= 0)
        o_c = jnp.einsum('bhgqc,bchd->bqhgd', p_c.astype(vc.dtype), vc)

        imp = jnp.einsum('bhgqc,cs->bhqs', p_c, overlap)
        blk_t = tq // SEL_BLOCK
        forced = ((blk_ids[None, :] == 0) | (blk_ids[None, :] == blk_t[:, None])
                  | (blk_ids[None, :] == blk_t[:, None] - 1))
        imp = jnp.where(forced, imp + FORCE_SCORE, imp)
        imp = jnp.where(blk_ids[None, :] > blk_t[:, None], NEG_INF, imp)
        _, top = lax.top_k(imp, n_sel)
        n_tok = n_sel * SEL_BLOCK
        kg = ks_blocks[b_i, h_i, top].reshape(b, hkv, QBLOCK, n_tok, dh)
        vg = vs_blocks[b_i, h_i, top].reshape(b, hkv, QBLOCK, n_tok, dh)
        pos_s = (top[..., None] * SEL_BLOCK + jnp.arange(SEL_BLOCK)).reshape(b, hkv, QBLOCK, n_tok)
        dist_s = tq[None, None, :, None] - pos_s
        s_s = jnp.einsum('bqhgd,bhqkd->bhgqk', qblk, kg).astype(jnp.float32) * scale
        bias_s = tbl_h[jnp.arange(hkv)[None, :, None, None, None],
                       jnp.arange(grp)[None, None, :, None, None],
                       t5_bucket(dist_s)[:, :, None]]
        p_s = masked_softmax(s_s + bias_s, (dist_s >= 0)[:, :, None])
        o_s = jnp.einsum('bhgqk,bhqkd->bqhgd', p_s.astype(vg.dtype), vg)

        kwin = lax.dynamic_slice_in_dim(kwp, t0, QBLOCK + NSA_WINDOW, 1)
        vwin = lax.dynamic_slice_in_dim(vwp, t0, QBLOCK + NSA_WINDOW, 1)
        pos_w = t0 - NSA_WINDOW + jnp.arange(QBLOCK + NSA_WINDOW)
        dist_w = tq[:, None] - pos_w[None, :]
        mask_w = (dist_w >= 0) & (dist_w < NSA_WINDOW) & (pos_w[None, :] >= 0)
        s_w = jnp.einsum('bqhgd,bkhd->bhgqk', qblk, kwin).astype(jnp.float32) * scale
        s_w = s_w + tbl_g[t5_bucket(dist_w)].transpose(2, 3, 0, 1).astype(jnp.float32)
        p_w = masked_softmax(s_w, mask_w)
        o_w = jnp.einsum('bhgqk,bkhd->bqhgd', p_w.astype(vwin.dtype), vwin)

        o = gblk[..., 0:1] * o_c + gblk[..., 1:2] * o_s + gblk[..., 2:3] * o_w
        return o.reshape(b, QBLOCK, hkv * grp * dh)

    out = lax.map(block, jnp.arange(t // QBLOCK))
    return out.transpose(1, 0, 2, 3).reshape(b, t, -1)


def swa_sink_attention(q, k, v, sinks, tbl):
    b, t = q.shape[0], q.shape[1]
    hkv, grp, dh = SWA_KV_HEADS, SWA_GROUP, HEAD_DIM
    scale = dh ** -0.5
    kp = jnp.pad(k, ((0, 0), (SWA_WINDOW, 0), (0, 0), (0, 0)))
    vp = jnp.pad(v, ((0, 0), (SWA_WINDOW, 0), (0, 0), (0, 0)))
    tbl_g = tbl.reshape(NUM_BUCKETS, hkv, grp)
    sink = sinks.reshape(hkv, grp)[None, :, :, None, None].astype(jnp.float32)

    def block(qb):
        t0 = qb * QBLOCK
        tq = t0 + jnp.arange(QBLOCK)
        qblk = lax.dynamic_slice_in_dim(q, t0, QBLOCK, 1).reshape(b, QBLOCK, hkv, grp, dh)
        kwin = lax.dynamic_slice_in_dim(kp, t0, QBLOCK + SWA_WINDOW, 1)
        vwin = lax.dynamic_slice_in_dim(vp, t0, QBLOCK + SWA_WINDOW, 1)
        pos = t0 - SWA_WINDOW + jnp.arange(QBLOCK + SWA_WINDOW)
        dist = tq[:, None] - pos[None, :]
        mask = (dist >= 0) & (dist < SWA_WINDOW) & (pos[None, :] >= 0)
        s = jnp.einsum('bqhgd,bkhd->bhgqk', qblk, kwin).astype(jnp.float32) * scale
        s = s + tbl_g[t5_bucket(dist)].transpose(2, 3, 0, 1).astype(jnp.float32)
        s = jnp.where(mask, s, NEG_INF)
        m = jnp.maximum(jnp.max(s, axis=-1, keepdims=True), sink)
        e = jnp.exp(s - m)
        p = e / (jnp.sum(e, axis=-1, keepdims=True) + jnp.exp(sink - m))
        o = jnp.einsum('bhgqk,bkhd->bqhgd', p.astype(vwin.dtype), vwin)
        return o.reshape(b, QBLOCK, hkv * grp * dh)

    out = lax.map(block, jnp.arange(t // QBLOCK))
    return out.transpose(1, 0, 2, 3).reshape(b, t, -1)


def setup_inputs(seed: int = 0) -> dict:
    key = jax.random.key(seed)
    ks = jax.random.split(key, 18)
    f32 = jnp.float32
    nrm = lambda k, shape, s: jax.random.normal(k, shape, f32) * s
    return {
        "x": nrm(ks[0], (BATCH, SEQ, D_MODEL), 1.0),
        "rel_bias": nrm(ks[1], (NUM_BUCKETS, N_BIAS_HEADS), 0.5),
        "norm_mix": 1.0 + nrm(ks[2], (DEPTH, D_MODEL), 0.02),
        "norm_ffn": 1.0 + nrm(ks[3], (DEPTH, D_MODEL), 0.02),
        "w_in": nrm(ks[4], (DEPTH, D_MODEL, PROJ_WIDTH), D_MODEL ** -0.5),
        "b_in": nrm(ks[5], (DEPTH, PROJ_WIDTH), 0.02),
        "cmp_pos_k": nrm(ks[6], (DEPTH, CMP_LEN, HEAD_DIM), 0.1),
        "cmp_pos_v": nrm(ks[7], (DEPTH, CMP_LEN, HEAD_DIM), 0.1),
        "cmp_k_w1": nrm(ks[8], (DEPTH, CMP_LEN * HEAD_DIM, CMP_HIDDEN), (CMP_LEN * HEAD_DIM) ** -0.5),
        "cmp_k_w2": nrm(ks[9], (DEPTH, CMP_HIDDEN, HEAD_DIM), CMP_HIDDEN ** -0.5),
        "cmp_v_w1": nrm(ks[10], (DEPTH, CMP_LEN * HEAD_DIM, CMP_HIDDEN), (CMP_LEN * HEAD_DIM) ** -0.5),
        "cmp_v_w2": nrm(ks[11], (DEPTH, CMP_HIDDEN, HEAD_DIM), CMP_HIDDEN ** -0.5),
        "sinks": nrm(ks[12], (DEPTH, SWA_HEADS), 1.0),
        "w_out": nrm(ks[13], (DEPTH, MIX_WIDTH, D_MODEL), MIX_WIDTH ** -0.5),
        "w_gate": nrm(ks[14], (DEPTH, D_MODEL, D_FF), D_MODEL ** -0.5),
        "w_up": nrm(ks[15], (DEPTH, D_MODEL, D_FF), D_MODEL ** -0.5),
        "w_down": nrm(ks[16], (DEPTH, D_FF, D_MODEL), D_FF ** -0.5),
        "norm_final": 1.0 + nrm(ks[17], (D_MODEL,), 0.02),
    }


def reference(x, rel_bias, norm_mix, norm_ffn, w_in, b_in, cmp_pos_k, cmp_pos_v,
              cmp_k_w1, cmp_k_w2, cmp_v_w1, cmp_v_w2, sinks, w_out, w_gate, w_up,
              w_down, norm_final):
    tbl_a = rel_bias[:, :NSA_HEADS]
    tbl_b = rel_bias[:, NSA_HEADS:]
    for layer in range(DEPTH):
        h = rms_norm(x, norm_mix[layer])
        proj = h @ w_in[layer] + b_in[layer]
        q_a, k_c, v_c, k_s, v_s, k_w, v_w, g_a, q_b, k_b, v_b = split_columns(proj)
        gates = jax.nn.sigmoid(g_a).reshape(g_a.shape[0], g_a.shape[1], NSA_HEADS, 3)
        o_a = nsa_attention(
            to_heads(q_a, NSA_HEADS),
            to_heads(k_c, NSA_KV_HEADS), to_heads(v_c, NSA_KV_HEADS),
            to_heads(k_s, NSA_KV_HEADS), to_heads(v_s, NSA_KV_HEADS),
            to_heads(k_w, NSA_KV_HEADS), to_heads(v_w, NSA_KV_HEADS),
            gates, tbl_a,
            cmp_pos_k[layer], cmp_pos_v[layer],
            cmp_k_w1[layer], cmp_k_w2[layer], cmp_v_w1[layer], cmp_v_w2[layer])
        o_b = swa_sink_attention(
            to_heads(q_b, SWA_HEADS), to_heads(k_b, SWA_KV_HEADS),
            to_heads(v_b, SWA_KV_HEADS), sinks[layer], tbl_b)
        x = x + jnp.concatenate([o_a, o_b], axis=-1) @ w_out[layer]
        h = rms_norm(x, norm_ffn[layer])
        x = x + (jax.nn.silu(h @ w_gate[layer]) * (h @ w_up[layer])) @ w_down[layer]
    return rms_norm(x, norm_final)
```

```python
import functools
import math

import jax
import jax.numpy as jnp
from jax import lax
from jax.experimental import pallas as pl
from jax.experimental.pallas import tpu as pltpu

F32 = jnp.float32
BF16 = jnp.bfloat16

D_MODEL = 2048
HEAD_DIM = 64
NSA_HEADS = 16
NSA_KV_HEADS = 4
NSA_GROUP = 4
SWA_HEADS = 16
SWA_KV_HEADS = 2
SWA_GROUP = 8
CMP_LEN = 32
CMP_STRIDE = 16
CMP_HIDDEN = 256
SEL_BLOCK = 64
SEL_SHIFT = 6
N_SEL = 16
NSA_WINDOW = 512
SWA_WINDOW = 128
NUM_BUCKETS = 32
REL_MAX_DISTANCE = 1024
D_FF = 5632
RMS_EPS = 1e-5
NEG_INF = -1e30
FORCE_SCORE = 1e6

QB = 128
KT = 128
N_NEAR = 8
N_WIN_TILES = NSA_WINDOW // KT + 1
PROJ_PAD = 3968
ROW_TILE = 512
FF_TILE = 512
VMEM_LIMIT = 56 * 1024 * 1024

OFF_QA = 0
OFF_KV = 1024
OFF_QB = 2560
OFF_KB = 3584
OFF_VB = 3712
OFF_GT = 3840


def _dot(a, b):
    return jnp.dot(a, b, preferred_element_type=F32)


def _dot_nt(a, b):
    return lax.dot_general(a, b, (((1,), (1,)), ((), ())), preferred_element_type=F32)


def _rms(x, g):
    return x * lax.rsqrt(jnp.mean(x * x, axis=-1, keepdims=True) + RMS_EPS) * g


def _inproj_kernel(x_ref, g_ref, w_ref, b_ref, qa_ref, kc_ref, vc_ref, ks_ref, vs_ref,
                   kw_ref, vw_ref, gt_ref, qb_ref, kb_ref, vb_ref):
    h = _rms(x_ref[0], g_ref[...]).astype(BF16)

    def proj(a, b):
        return _dot(h, w_ref[:, a:b]) + b_ref[:, a:b]

    qa_ref[0] = proj(OFF_QA, OFF_QA + 1024).astype(qa_ref.dtype)
    for k, ref in enumerate((kc_ref, vc_ref, ks_ref, vs_ref, kw_ref, vw_ref)):
        r = proj(OFF_KV + 256 * k, OFF_KV + 256 * (k + 1))
        for hh in range(NSA_KV_HEADS):
            ref[0, hh] = r[:, HEAD_DIM * hh:HEAD_DIM * (hh + 1)].astype(ref.dtype)
    qb_ref[0] = proj(OFF_QB, OFF_QB + 1024).astype(qb_ref.dtype)
    for off, ref in ((OFF_KB, kb_ref), (OFF_VB, vb_ref)):
        r = proj(off, off + 128)
        for hh in range(SWA_KV_HEADS):
            ref[0, hh] = r[:, HEAD_DIM * hh:HEAD_DIM * (hh + 1)].astype(ref.dtype)
    r = proj(OFF_GT, OFF_GT + 128)
    gt_ref[0, 0] = r
    for hh in range(1, NSA_KV_HEADS):
        gt_ref[0, hh] = pltpu.roll(r, 128 - 3 * NSA_GROUP * hh, axis=1)


def _inproj(x, g, w, b):
    bsz, t, d = x.shape
    tm = min(ROW_TILE, t)
    kv4 = lambda dt: jax.ShapeDtypeStruct((bsz, NSA_KV_HEADS, t, HEAD_DIM), dt)
    kv2 = jax.ShapeDtypeStruct((bsz, SWA_KV_HEADS, t, HEAD_DIM), BF16)
    wide = jax.ShapeDtypeStruct((bsz, t, 1024), BF16)
    out_shape = (wide, kv4(F32), kv4(F32), kv4(BF16), kv4(BF16), kv4(BF16), kv4(BF16),
                 jax.ShapeDtypeStruct((bsz, NSA_KV_HEADS, t, 128), F32), wide, kv2, kv2)
    wide_spec = pl.BlockSpec((1, tm, 1024), lambda bb, i: (bb, i, 0))
    kv4_spec = pl.BlockSpec((1, NSA_KV_HEADS, tm, HEAD_DIM), lambda bb, i: (bb, 0, i, 0))
    kv2_spec = pl.BlockSpec((1, SWA_KV_HEADS, tm, HEAD_DIM), lambda bb, i: (bb, 0, i, 0))
    gt_spec = pl.BlockSpec((1, NSA_KV_HEADS, tm, 128), lambda bb, i: (bb, 0, i, 0))
    return pl.pallas_call(
        _inproj_kernel,
        grid=(bsz, t // tm),
        in_specs=[
            pl.BlockSpec((1, tm, d), lambda bb, i: (bb, i, 0)),
            pl.BlockSpec((1, d), lambda bb, i: (0, 0)),
            pl.BlockSpec((d, PROJ_PAD), lambda bb, i: (0, 0), pipeline_mode=pl.Buffered(1)),
            pl.BlockSpec((1, PROJ_PAD), lambda bb, i: (0, 0)),
        ],
        out_specs=(wide_spec, kv4_spec, kv4_spec, kv4_spec, kv4_spec, kv4_spec, kv4_spec,
                   gt_spec, wide_spec, kv2_spec, kv2_spec),
        out_shape=out_shape,
        compiler_params=pltpu.CompilerParams(
            dimension_semantics=("parallel", "parallel"), vmem_limit_bytes=VMEM_LIMIT),
        name="inproj",
    )(x, g, w, b)


def _compress_kernel(kr_ref, vr_ref, pk_ref, pv_ref, k1_ref, k2_ref, v1_ref, v2_ref,
                     ko_ref, vo_ref):
    n_chunk = kr_ref.shape[2] // CMP_STRIDE
    half = CMP_STRIDE * HEAD_DIM
    for r_ref, p_ref, w1_ref, w2_ref, o_ref in ((kr_ref, pk_ref, k1_ref, k2_ref, ko_ref),
                                                (vr_ref, pv_ref, v1_ref, v2_ref, vo_ref)):
        top = jnp.zeros((n_chunk, CMP_HIDDEN), F32)
        bot = jnp.zeros((n_chunk, CMP_HIDDEN), F32)
        for l in range(CMP_STRIDE):
            xl = r_ref[0, 0, pl.ds(l, n_chunk, stride=CMP_STRIDE), :]
            a = (xl + p_ref[l:l + 1, :]).astype(BF16)
            top = top + _dot(a, w1_ref[HEAD_DIM * l:HEAD_DIM * (l + 1), :])
            c = (xl + p_ref[CMP_STRIDE + l:CMP_STRIDE + l + 1, :]).astype(BF16)
            bot = bot + _dot(c, w1_ref[half + HEAD_DIM * l:half + HEAD_DIM * (l + 1), :])
        hid = top + pltpu.roll(bot, n_chunk - 1, axis=0)
        out = _dot(jax.nn.gelu(hid).astype(BF16), w2_ref[...])
        row = lax.broadcasted_iota(jnp.int32, out.shape, 0)
        o_ref[0, 0] = jnp.where(row < n_chunk - 1, out, 0.0).astype(o_ref.dtype)


def _compress(kc_raw, vc_raw, pos_k, pos_v, k1, k2, v1, v2):
    bsz, hkv, t, dh = kc_raw.shape
    n_chunk = t // CMP_STRIDE
    raw_spec = pl.BlockSpec((1, 1, t, dh), lambda bb, hh: (bb, hh, 0, 0))
    full = lambda a: pl.BlockSpec(a.shape, lambda bb, hh: (0,) * a.ndim)
    out_spec = pl.BlockSpec((1, 1, n_chunk, dh), lambda bb, hh: (bb, hh, 0, 0))
    out = jax.ShapeDtypeStruct((bsz, hkv, n_chunk, dh), BF16)
    return pl.pallas_call(
        _compress_kernel,
        grid=(bsz, hkv),
        in_specs=[raw_spec, raw_spec, full(pos_k), full(pos_v), full(k1), full(k2), full(v1), full(v2)],
        out_specs=(out_spec, out_spec),
        out_shape=(out, out),
        compiler_params=pltpu.CompilerParams(
            dimension_semantics=("parallel", "parallel"), vmem_limit_bytes=VMEM_LIMIT),
        name="compress",
    )(kc_raw, vc_raw, pos_k, pos_v, k1, k2, v1, v2)


def _flash_init(m_scr, l_scr, acc_scr):
    m_scr[...] = jnp.full(m_scr.shape, NEG_INF, F32)
    l_scr[...] = jnp.zeros(l_scr.shape, F32)
    acc_scr[...] = jnp.zeros(acc_scr.shape, F32)


def _flash_update(s, v, m_scr, l_scr, acc_scr):
    m_prev = m_scr[...]
    m_new = jnp.maximum(m_prev, jnp.max(s, axis=1, keepdims=True))
    alpha = jnp.exp(m_prev - m_new)
    p = jnp.exp(s - m_new)
    l_scr[...] = alpha * l_scr[...] + jnp.sum(p, axis=1, keepdims=True)
    acc_scr[...] = acc_scr[...] * alpha[:, :HEAD_DIM] + _dot(p.astype(BF16), v)
    m_scr[...] = m_new


def _flash_result(l_scr, acc_scr):
    return acc_scr[...] * (1.0 / l_scr[...])[:, :HEAD_DIM]


def _nsa_kernel(qa_ref, gt_ref, kc_ref, vc_ref, ks_ref, vs_ref, kw_ref, vw_ref,
                bc_ref, sb_ref, wb_ref, ov_ref, o_ref, m_scr, l_scr, acc_scr):
    qi = pl.program_id(2)
    t0 = qi * QB
    n_cmp = kc_ref.shape[2]
    grp = NSA_GROUP
    q = qa_ref[0] * jnp.asarray(HEAD_DIM ** -0.5, BF16)
    qs = jnp.concatenate([q[:, HEAD_DIM * g:HEAD_DIM * (g + 1)] for g in range(grp)], axis=0)

    s_all = _dot_nt(qs, kc_ref[0, 0])
    ii = lax.broadcasted_iota(jnp.int32, (QB, n_cmp), 0)
    cc = lax.broadcasted_iota(jnp.int32, (QB, n_cmp), 1)
    valid_c = (t0 + ii) >= (cc * CMP_STRIDE + CMP_LEN - 1)
    width = bc_ref.shape[2]
    shift = lax.rem(width - (n_cmp - 8) + 8 * qi, width)
    vc = vc_ref[0, 0]
    o_c = []
    p_sum = jnp.zeros((QB, n_cmp), F32)
    for g in range(grp):
        bias = pltpu.roll(bc_ref[g], shift, axis=1)[:, :n_cmp]
        s = jnp.where(valid_c, s_all[QB * g:QB * (g + 1)] + bias, NEG_INF)
        e = jnp.exp(s - jnp.max(s, axis=1, keepdims=True))
        p = jnp.where(valid_c, e * (1.0 / jnp.sum(e, axis=1, keepdims=True)), 0.0)
        o_c.append(_dot(p.astype(BF16), vc))
        p_sum = p_sum + p

    p_hi = p_sum.astype(BF16)
    p_lo = (p_sum - p_hi.astype(F32)).astype(BF16)
    imp = _dot(p_hi, ov_ref[...]) + _dot(p_lo, ov_ref[...])
    n_blk = ks_ref.shape[2] // SEL_BLOCK
    imp_t = imp.T[:n_blk]
    jb = lax.broadcasted_iota(jnp.int32, (n_blk, QB), 0)
    lane = lax.broadcasted_iota(jnp.int32, (n_blk, QB), 1)
    blk_t = (t0 + lane) >> SEL_SHIFT
    forced = (jb == 0) | (jb == blk_t) | (jb == blk_t - 1)
    val = jnp.where(forced, imp_t + FORCE_SCORE, imp_t)
    val = jnp.where(jb > blk_t, NEG_INF, val)
    rank = jnp.zeros((n_blk, QB), jnp.int32)
    for j in range(n_blk):
        r = jnp.broadcast_to(val[j:j + 1, :], (n_blk, QB))
        beats = (r > val) | ((r == val) & (jb > j))
        rank = rank + beats.astype(jnp.int32)
    sel_t = jnp.where(rank < min(N_SEL, n_blk), 1.0, 0.0)
    if n_blk < 128:
        sel_t = jnp.concatenate([sel_t, jnp.zeros((128 - n_blk, QB), F32)], axis=0)
    sel = sel_t.T.astype(BF16)

    er = lax.broadcasted_iota(jnp.int32, (128, KT), 0)
    ec = lax.broadcasted_iota(jnp.int32, (128, KT), 1) >> SEL_SHIFT
    _flash_init(m_scr, l_scr, acc_scr)

    def sel_body(j, carry):
        rows = pl.ds(pl.multiple_of(j * KT, KT), KT)
        s = _dot_nt(qs, ks_ref[0, 0, rows, :])
        d = jnp.minimum(qi - j, N_NEAR)
        s = s + sb_ref[:, pl.ds(d, 1)].reshape(grp * QB, KT)
        expand = jnp.where(er == ec + j * (KT // SEL_BLOCK), 1.0, 0.0).astype(BF16)
        keep = _dot(sel, expand) > 0.5
        s = jnp.where(jnp.concatenate([keep] * grp, axis=0), s, NEG_INF)
        _flash_update(s, vs_ref[0, 0, rows, :], m_scr, l_scr, acc_scr)
        return carry

    lax.fori_loop(0, qi + 1, sel_body, 0)
    o_s = _flash_result(l_scr, acc_scr)

    _flash_init(m_scr, l_scr, acc_scr)

    def win_body(d, carry):
        rows = pl.ds(pl.multiple_of((qi - d) * KT, KT), KT)
        s = _dot_nt(qs, kw_ref[0, 0, rows, :]) + wb_ref[:, pl.ds(d, 1)].reshape(grp * QB, KT)
        _flash_update(s, vw_ref[0, 0, rows, :], m_scr, l_scr, acc_scr)
        return carry

    lax.fori_loop(0, jnp.minimum(qi, N_WIN_TILES - 1) + 1, win_body, 0)
    o_w = _flash_result(l_scr, acc_scr)

    gate = jax.nn.sigmoid(gt_ref[0, 0])
    outs = []
    for g in range(grp):
        rows = slice(QB * g, QB * (g + 1))
        outs.append(gate[:, 3 * g:3 * g + 1] * o_c[g]
                    + gate[:, 3 * g + 1:3 * g + 2] * o_s[rows]
                    + gate[:, 3 * g + 2:3 * g + 3] * o_w[rows])
    o_ref[0] = jnp.concatenate(outs, axis=1).astype(o_ref.dtype)


def _nsa_attention(qa, gt, kc, vc, ks, vs, kw, vw, bc, sb, wb, ov):
    bsz, t, _ = qa.shape
    hkv = NSA_KV_HEADS
    n_cmp = kc.shape[2]
    gw = NSA_GROUP * HEAD_DIM
    kv_spec = pl.BlockSpec((1, 1, t, HEAD_DIM), lambda bb, hh, i: (bb, hh, 0, 0))
    cmp_spec = pl.BlockSpec((1, 1, n_cmp, HEAD_DIM), lambda bb, hh, i: (bb, hh, 0, 0))
    tbl = lambda a: pl.BlockSpec((NSA_GROUP,) + a.shape[1:], lambda bb, hh, i: (hh,) + (0,) * (a.ndim - 1))
    return pl.pallas_call(
        _nsa_kernel,
        grid=(bsz, hkv, t // QB),
        in_specs=[
            pl.BlockSpec((1, QB, gw), lambda bb, hh, i: (bb, i, hh)),
            pl.BlockSpec((1, 1, QB, 128), lambda bb, hh, i: (bb, hh, i, 0)),
            cmp_spec, cmp_spec, kv_spec, kv_spec, kv_spec, kv_spec,
            tbl(bc), tbl(sb), tbl(wb),
            pl.BlockSpec(ov.shape, lambda bb, hh, i: (0, 0)),
        ],
        out_specs=pl.BlockSpec((1, QB, gw), lambda bb, hh, i: (bb, i, hh)),
        out_shape=jax.ShapeDtypeStruct((bsz, t, NSA_HEADS * HEAD_DIM), BF16),
        scratch_shapes=[pltpu.VMEM((NSA_GROUP * QB, 128), F32),
                        pltpu.VMEM((NSA_GROUP * QB, 128), F32),
                        pltpu.VMEM((NSA_GROUP * QB, HEAD_DIM), F32)],
        compiler_params=pltpu.CompilerParams(
            dimension_semantics=("parallel", "parallel", "arbitrary"), vmem_limit_bytes=VMEM_LIMIT),
        name="nsa_attention",
    )(qa, gt, kc, vc, ks, vs, kw, vw, bc, sb, wb, ov)


def _swa_kernel(sink_ref, qb_ref, kp_ref, kn_ref, vp_ref, vn_ref, wb_ref, o_ref):
    hk = pl.program_id(1)
    qi = pl.program_id(2)
    q = qb_ref[0] * jnp.asarray(HEAD_DIM ** -0.5, BF16)
    k = jnp.concatenate([kp_ref[0, 0], kn_ref[0, 0]], axis=0)
    v = jnp.concatenate([vp_ref[0, 0], vn_ref[0, 0]], axis=0)
    col = lax.broadcasted_iota(jnp.int32, (QB, 2 * KT), 1)
    in_seq = (col >= KT) | (qi > 0)
    outs = []
    for g in range(SWA_GROUP):
        s = _dot_nt(q[:, HEAD_DIM * g:HEAD_DIM * (g + 1)], k) + wb_ref[g]
        s = jnp.where(in_seq, s, NEG_INF)
        sink = sink_ref[hk * SWA_GROUP + g]
        m = jnp.maximum(jnp.max(s, axis=1, keepdims=True), sink)
        e = jnp.exp(s - m)
        p = e * (1.0 / (jnp.sum(e, axis=1, keepdims=True) + jnp.exp(sink - m)))
        outs.append(_dot(p.astype(BF16), v))
    o_ref[0] = jnp.concatenate(outs, axis=1).astype(o_ref.dtype)


def _swa_attention(sinks, qb, kb, vb, wbb):
    bsz, t, _ = qb.shape
    gw = SWA_GROUP * HEAD_DIM
    prev = pl.BlockSpec((1, 1, KT, HEAD_DIM), lambda bb, hh, i: (bb, hh, jnp.maximum(i - 1, 0), 0))
    cur = pl.BlockSpec((1, 1, KT, HEAD_DIM), lambda bb, hh, i: (bb, hh, i, 0))
    return pl.pallas_call(
        _swa_kernel,
        grid=(bsz, SWA_KV_HEADS, t // QB),
        in_specs=[
            pl.BlockSpec(memory_space=pltpu.SMEM),
            pl.BlockSpec((1, QB, gw), lambda bb, hh, i: (bb, i, hh)),
            prev, cur, prev, cur,
            pl.BlockSpec((SWA_GROUP, QB, 2 * KT), lambda bb, hh, i: (hh, 0, 0)),
        ],
        out_specs=pl.BlockSpec((1, QB, gw), lambda bb, hh, i: (bb, i, hh)),
        out_shape=jax.ShapeDtypeStruct((bsz, t, SWA_HEADS * HEAD_DIM), BF16),
        compiler_params=pltpu.CompilerParams(
            dimension_semantics=("parallel", "parallel", "arbitrary"), vmem_limit_bytes=VMEM_LIMIT),
        name="swa_attention",
    )(sinks, qb, kb, kb, vb, vb, wbb)


def _outproj_kernel(x_ref, oa_ref, ob_ref, w_ref, o_ref):
    na = oa_ref.shape[1]
    o_ref[...] = x_ref[...] + _dot(oa_ref[...], w_ref[:na, :]) + _dot(ob_ref[...], w_ref[na:, :])


def _outproj(x, oa, ob, w):
    n, d = x.shape
    tm = min(ROW_TILE, n)
    row = lambda width: pl.BlockSpec((tm, width), lambda i: (i, 0))
    return pl.pallas_call(
        _outproj_kernel,
        grid=(n // tm,),
        in_specs=[row(d), row(oa.shape[1]), row(ob.shape[1]), pl.BlockSpec(w.shape, lambda i: (0, 0))],
        out_specs=row(d),
        out_shape=jax.ShapeDtypeStruct((n, d), F32),
        compiler_params=pltpu.CompilerParams(
            dimension_semantics=("parallel",), vmem_limit_bytes=VMEM_LIMIT),
        name="outproj",
    )(x, oa, ob, w)


def _ffn_kernel(x_ref, g_ref, wg_ref, wu_ref, wd_ref, gf_ref, o_ref, h_scr, acc_scr, *, final_norm):
    f = pl.program_id(1)

    @pl.when(f == 0)
    def _():
        h_scr[...] = _rms(x_ref[...], g_ref[...]).astype(BF16)
        acc_scr[...] = jnp.zeros(acc_scr.shape, F32)

    h = h_scr[...]
    gate = _dot(h, wg_ref[...])
    act = (gate * jax.nn.sigmoid(gate) * _dot(h, wu_ref[...])).astype(BF16)
    acc_scr[...] += _dot(act, wd_ref[...])

    @pl.when(f == pl.num_programs(1) - 1)
    def _():
        y = x_ref[...] + acc_scr[...]
        o_ref[...] = _rms(y, gf_ref[...]) if final_norm else y


def _ffn(x, g, wg, wu, wd, g_final, final_norm):
    n, d = x.shape
    dff = wg.shape[1]
    tm = min(ROW_TILE, n)
    return pl.pallas_call(
        functools.partial(_ffn_kernel, final_norm=final_norm),
        grid=(n // tm, dff // FF_TILE),
        in_specs=[
            pl.BlockSpec((tm, d), lambda i, f: (i, 0)),
            pl.BlockSpec((1, d), lambda i, f: (0, 0)),
            pl.BlockSpec((d, FF_TILE), lambda i, f: (0, f)),
            pl.BlockSpec((d, FF_TILE), lambda i, f: (0, f)),
            pl.BlockSpec((FF_TILE, d), lambda i, f: (f, 0)),
            pl.BlockSpec((1, d), lambda i, f: (0, 0)),
        ],
        out_specs=pl.BlockSpec((tm, d), lambda i, f: (i, 0)),
        out_shape=jax.ShapeDtypeStruct((n, d), F32),
        scratch_shapes=[pltpu.VMEM((tm, d), BF16), pltpu.VMEM((tm, d), F32)],
        compiler_params=pltpu.CompilerParams(
            dimension_semantics=("parallel", "arbitrary"), vmem_limit_bytes=VMEM_LIMIT),
        name="ffn",
    )(x, g, wg, wu, wd, g_final)


def _t5_bucket(dist):
    dist = jnp.maximum(dist, 0)
    max_exact = NUM_BUCKETS // 2
    d = jnp.maximum(dist, 1).astype(F32)
    ratio = jnp.log(d / max_exact) / math.log(REL_MAX_DISTANCE / max_exact)
    large = max_exact + (ratio * (NUM_BUCKETS - max_exact)).astype(jnp.int32)
    large = jnp.minimum(large, NUM_BUCKETS - 1)
    return jnp.where(dist < max_exact, dist, large)


def _bias_tables(rel_bias, n_cmp):
    tbl_a = rel_bias[:, :NSA_HEADS].astype(F32)
    tbl_b = rel_bias[:, NSA_HEADS:].astype(F32)
    i = jnp.arange(QB)[:, None]
    j = jnp.arange(KT)[None, :]
    heads_first = lambda a: jnp.moveaxis(a, -1, 0)
    dist = KT * jnp.arange(N_NEAR + 1)[:, None, None] + i - j
    near = heads_first(tbl_a[_t5_bucket(dist)])
    sb = jnp.where(dist >= 0, near, NEG_INF)
    dw = dist[:N_WIN_TILES]
    wb = jnp.where((dw >= 0) & (dw < NSA_WINDOW), near[:, :N_WIN_TILES], NEG_INF)
    width = 2 * n_cmp
    cprime = jnp.arange(width)[None, :] - (n_cmp - 8)
    bc = heads_first(tbl_a[_t5_bucket(i - CMP_STRIDE * cprime - (CMP_LEN - 1))])
    c2 = jnp.arange(2 * KT)[None, :]
    db = i + KT - c2
    wbb = jnp.where((db >= 0) & (db < SWA_WINDOW), heads_first(tbl_b[_t5_bucket(db)]), NEG_INF)
    return bc, sb, wb, wbb


def _overlap_matrix(n_cmp, n_blk):
    ci = jnp.arange(n_cmp)[:, None] * CMP_STRIDE
    sj = jnp.arange(128)[None, :] * SEL_BLOCK
    ov = (ci < sj + SEL_BLOCK) & (ci + CMP_LEN > sj) & (jnp.arange(128)[None, :] < n_blk)
    ov = ov & (jnp.arange(n_cmp)[:, None] < n_cmp - 1)
    return ov.astype(BF16)


def _reorder_proj(w):
    pad = jnp.zeros(w.shape[:-1] + (PROJ_PAD - w.shape[-1],), w.dtype)
    return jnp.concatenate([w[..., :2560], w[..., 2608:], w[..., 2560:2608], pad], axis=-1)


def kernel(x, rel_bias, norm_mix, norm_ffn, w_in, b_in, cmp_pos_k, cmp_pos_v, cmp_k_w1, cmp_k_w2,
           cmp_v_w1, cmp_v_w2, sinks, w_out, w_gate, w_up, w_down, norm_final):
    bsz, t, d = x.shape
    depth = w_in.shape[0]
    n_cmp = t // CMP_STRIDE
    bc, sb, wb, wbb = _bias_tables(rel_bias, n_cmp)
    ov = _overlap_matrix(n_cmp, t // SEL_BLOCK)
    w_in_r = _reorder_proj(w_in).astype(BF16)
    b_in_r = _reorder_proj(b_in).astype(F32)
    for layer in range(depth):
        qa, kc_raw, vc_raw, ks, vs, kw, vw, gt, qb, kb, vb = _inproj(
            x, norm_mix[layer][None, :], w_in_r[layer], b_in_r[layer][None, :])
        kc, vc = _compress(kc_raw, vc_raw, cmp_pos_k[layer], cmp_pos_v[layer],
                           cmp_k_w1[layer].astype(BF16), cmp_k_w2[layer].astype(BF16),
                           cmp_v_w1[layer].astype(BF16), cmp_v_w2[layer].astype(BF16))
        o_a = _nsa_attention(qa, gt, kc, vc, ks, vs, kw, vw, bc, sb, wb, ov)
        o_b = _swa_attention(sinks[layer], qb, kb, vb, wbb)
        x2 = _outproj(x.reshape(bsz * t, d), o_a.reshape(bsz * t, -1), o_b.reshape(bsz * t, -1),
                      w_out[layer].astype(BF16))
        x2 = _ffn(x2, norm_ffn[layer][None, :], w_gate[layer].astype(BF16), w_up[layer].astype(BF16),
                  w_down[layer].astype(BF16), norm_final[None, :], layer == depth - 1)
        x = x2.reshape(bsz, t, d)
    return x
```

```python
import functools
import math

import jax
import jax.numpy as jnp
from jax import lax
from jax.experimental import pallas as pl
from jax.experimental.pallas import tpu as pltpu

F32 = jnp.float32
BF16 = jnp.bfloat16

D_MODEL = 2048
HEAD_DIM = 64
NSA_HEADS = 16
NSA_KV_HEADS = 4
NSA_GROUP = 4
SWA_HEADS = 16
SWA_KV_HEADS = 2
SWA_GROUP = 8
CMP_LEN = 32
CMP_STRIDE = 16
CMP_HIDDEN = 256
SEL_BLOCK = 64
SEL_SHIFT = 6
N_SEL = 16
NSA_WINDOW = 512
SWA_WINDOW = 128
NUM_BUCKETS = 32
REL_MAX_DISTANCE = 1024
RMS_EPS = 1e-5
NEG_INF = -1e30
FORCE_SCORE = 1e6

QB = 128
KT = 128
N_NEAR = 8
N_WIN_TILES = NSA_WINDOW // KT + 1
GATE_ROWS = 16
MASKED = NUM_BUCKETS
ROW_TILE = 512
FF_TILE = 512
VMEM_LIMIT = 56 * 1024 * 1024

COL_QA = (0, 1024)
COL_KC, COL_VC = (1024, 1280), (1280, 1536)
COL_KS, COL_VS = (1536, 1792), (1792, 2048)
COL_KW, COL_VW = (2048, 2304), (2304, 2560)
COL_GT = (2560, 2608)
COL_QB = (2608, 3632)
COL_KB, COL_VB = (3632, 3760), (3760, 3888)
N_ROWMAJOR = 4 * 256 + 1024 + 2 * 128
N_COLMAJOR = 1024 + 2 * 256 + NSA_KV_HEADS * GATE_ROWS


def _dot(a, b):
    return jnp.dot(a, b, preferred_element_type=F32)


def _dot_nt(a, b):
    return lax.dot_general(a, b, (((1,), (1,)), ((), ())), preferred_element_type=F32)


def _rms(x, g):
    return x * lax.rsqrt(jnp.mean(x * x, axis=-1, keepdims=True) + RMS_EPS) * g


def _inproj_kernel(x_ref, g_ref, wn_ref, bn_ref, wt_ref, bt_ref,
                   qat_ref, kc_ref, vc_ref, ks_ref, kw_ref, vst_ref, vwt_ref, gtt_ref,
                   qb_ref, kb_ref, vb_ref):
    h = _rms(x_ref[0], g_ref[...]).astype(BF16)

    def rowmajor(a, b):
        return _dot(h, wn_ref[:, a:b]) + bn_ref[:, a:b]

    def colmajor(a, b):
        return _dot_nt(wt_ref[a:b, :], h) + bt_ref[a:b, :]

    for k, ref in enumerate((kc_ref, vc_ref, ks_ref, kw_ref)):
        r = rowmajor(256 * k, 256 * (k + 1))
        for hh in range(NSA_KV_HEADS):
            ref[0, hh] = r[:, HEAD_DIM * hh:HEAD_DIM * (hh + 1)].astype(ref.dtype)
    qb_ref[0] = rowmajor(1024, 2048).astype(qb_ref.dtype)
    for off, ref in ((2048, kb_ref), (2176, vb_ref)):
        r = rowmajor(off, off + 128)
        for hh in range(SWA_KV_HEADS):
            ref[0, hh] = r[:, HEAD_DIM * hh:HEAD_DIM * (hh + 1)].astype(ref.dtype)
    qat_ref[0] = (colmajor(0, 1024) * (HEAD_DIM ** -0.5)).astype(qat_ref.dtype)
    vst_ref[0] = colmajor(1024, 1280).astype(vst_ref.dtype)
    vwt_ref[0] = colmajor(1280, 1536).astype(vwt_ref.dtype)
    gtt_ref[0] = colmajor(1536, N_COLMAJOR)


def _inproj(x, g, wn, bn, wt, bt):
    bsz, t, d = x.shape
    tm = min(ROW_TILE, t)
    kv4 = lambda dt: jax.ShapeDtypeStruct((bsz, NSA_KV_HEADS, t, HEAD_DIM), dt)
    kv2 = jax.ShapeDtypeStruct((bsz, SWA_KV_HEADS, t, HEAD_DIM), BF16)
    colm = lambda rows, dt: jax.ShapeDtypeStruct((bsz, rows, t), dt)
    out_shape = (colm(1024, BF16), kv4(F32), kv4(F32), kv4(BF16), kv4(BF16),
                 colm(256, BF16), colm(256, BF16), colm(NSA_KV_HEADS * GATE_ROWS, F32),
                 jax.ShapeDtypeStruct((bsz, t, 1024), BF16), kv2, kv2)
    kv4_spec = pl.BlockSpec((1, NSA_KV_HEADS, tm, HEAD_DIM), lambda bb, i: (bb, 0, i, 0))
    kv2_spec = pl.BlockSpec((1, SWA_KV_HEADS, tm, HEAD_DIM), lambda bb, i: (bb, 0, i, 0))
    colm_spec = lambda rows: pl.BlockSpec((1, rows, tm), lambda bb, i: (bb, 0, i))
    const = lambda a: pl.BlockSpec(a.shape, lambda bb, i: (0, 0), pipeline_mode=pl.Buffered(1))
    return pl.pallas_call(
        _inproj_kernel,
        grid=(bsz, t // tm),
        in_specs=[pl.BlockSpec((1, tm, d), lambda bb, i: (bb, i, 0)),
                  const(g), const(wn), const(bn), const(wt), const(bt)],
        out_specs=(colm_spec(1024), kv4_spec, kv4_spec, kv4_spec, kv4_spec,
                   colm_spec(256), colm_spec(256), colm_spec(NSA_KV_HEADS * GATE_ROWS),
                   pl.BlockSpec((1, tm, 1024), lambda bb, i: (bb, i, 0)), kv2_spec, kv2_spec),
        out_shape=out_shape,
        compiler_params=pltpu.CompilerParams(
            dimension_semantics=("parallel", "parallel"), vmem_limit_bytes=VMEM_LIMIT),
        name="inproj",
    )(x, g, wn, bn, wt, bt)


def _compress_kernel(kr_ref, vr_ref, pk_ref, pv_ref, k1_ref, k2_ref, v1_ref, v2t_ref,
                     ko_ref, vot_ref):
    n_chunk = kr_ref.shape[2] // CMP_STRIDE
    half = CMP_STRIDE * HEAD_DIM

    def hidden(r_ref, p_ref, w1_ref):
        top = jnp.zeros((n_chunk, CMP_HIDDEN), F32)
        bot = jnp.zeros((n_chunk, CMP_HIDDEN), F32)
        for l in range(CMP_STRIDE):
            xl = r_ref[0, 0, pl.ds(l, n_chunk, stride=CMP_STRIDE), :]
            a = (xl + p_ref[l:l + 1, :]).astype(BF16)
            top = top + _dot(a, w1_ref[HEAD_DIM * l:HEAD_DIM * (l + 1), :])
            c = (xl + p_ref[CMP_STRIDE + l:CMP_STRIDE + l + 1, :]).astype(BF16)
            bot = bot + _dot(c, w1_ref[half + HEAD_DIM * l:half + HEAD_DIM * (l + 1), :])
        hid = top + pltpu.roll(bot, n_chunk - 1, axis=0)
        return jax.nn.gelu(hid).astype(BF16)

    out = _dot(hidden(kr_ref, pk_ref, k1_ref), k2_ref[...])
    row = lax.broadcasted_iota(jnp.int32, out.shape, 0)
    ko_ref[0, 0] = jnp.where(row < n_chunk - 1, out, 0.0).astype(ko_ref.dtype)
    out_t = _dot_nt(v2t_ref[...], hidden(vr_ref, pv_ref, v1_ref))
    col = lax.broadcasted_iota(jnp.int32, out_t.shape, 1)
    vot_ref[0, 0] = jnp.where(col < n_chunk - 1, out_t, 0.0).astype(vot_ref.dtype)


def _compress(kc_raw, vc_raw, pos_k, pos_v, k1, k2, v1, v2t):
    bsz, hkv, t, dh = kc_raw.shape
    n_chunk = t // CMP_STRIDE
    raw_spec = pl.BlockSpec((1, 1, t, dh), lambda bb, hh: (bb, hh, 0, 0))
    full = lambda a: pl.BlockSpec(a.shape, lambda bb, hh: (0,) * a.ndim)
    return pl.pallas_call(
        _compress_kernel,
        grid=(bsz, hkv),
        in_specs=[raw_spec, raw_spec, full(pos_k), full(pos_v), full(k1), full(k2), full(v1), full(v2t)],
        out_specs=(pl.BlockSpec((1, 1, n_chunk, dh), lambda bb, hh: (bb, hh, 0, 0)),
                   pl.BlockSpec((1, 1, dh, n_chunk), lambda bb, hh: (bb, hh, 0, 0))),
        out_shape=(jax.ShapeDtypeStruct((bsz, hkv, n_chunk, dh), BF16),
                   jax.ShapeDtypeStruct((bsz, hkv, dh, n_chunk), BF16)),
        compiler_params=pltpu.CompilerParams(
            dimension_semantics=("parallel", "parallel"), vmem_limit_bytes=VMEM_LIMIT),
        name="compress",
    )(kc_raw, vc_raw, pos_k, pos_v, k1, k2, v1, v2t)


def _expand_kernel(tbl_ref, idx_ref, o_ref, *, gsz):
    grp = pl.program_id(0)
    idx = idx_ref[0]
    w = idx.shape[1]
    accs = [jnp.full(idx.shape, tbl_ref[MASKED, grp * gsz + g], F32) for g in range(gsz)]
    for b in range(NUM_BUCKETS):
        hit = idx == b
        for g in range(gsz):
            accs[g] = jnp.where(hit, tbl_ref[b, grp * gsz + g], accs[g])
    for g in range(gsz):
        o_ref[0, 0, :, w * g:w * (g + 1)] = accs[g]


def _expand_tables(tbl, idx, gsz):
    r0, r1, w = idx.shape
    heads = tbl.shape[1]
    tr = min(r1, 128)
    tbl_ext = jnp.concatenate([tbl.astype(F32), jnp.full((1, heads), NEG_INF, F32)], axis=0)
    return pl.pallas_call(
        functools.partial(_expand_kernel, gsz=gsz),
        grid=(heads // gsz, r0, r1 // tr),
        in_specs=[pl.BlockSpec(memory_space=pltpu.SMEM),
                  pl.BlockSpec((1, tr, w), lambda gg, a, r: (a, r, 0))],
        out_specs=pl.BlockSpec((1, 1, tr, gsz * w), lambda gg, a, r: (gg, a, r, 0)),
        out_shape=jax.ShapeDtypeStruct((heads // gsz, r0, r1, gsz * w), F32),
        compiler_params=pltpu.CompilerParams(
            dimension_semantics=("parallel", "parallel", "parallel"), vmem_limit_bytes=VMEM_LIMIT),
        name="expand_bias",
    )(tbl_ext, idx)


def _t5_bucket(dist):
    dist = jnp.maximum(dist, 0)
    max_exact = NUM_BUCKETS // 2
    d = jnp.maximum(dist, 1).astype(F32)
    ratio = jnp.log(d / max_exact) / math.log(REL_MAX_DISTANCE / max_exact)
    large = max_exact + (ratio * (NUM_BUCKETS - max_exact)).astype(jnp.int32)
    large = jnp.minimum(large, NUM_BUCKETS - 1)
    return jnp.where(dist < max_exact, dist, large)


def _bias_indices(n_cmp):
    kj = jnp.arange(KT)[:, None]
    qq = jnp.arange(QB)[None, :]
    d = jnp.arange(-1, N_NEAR + 1)[:, None, None]
    dist = KT * d + qq - kj
    sel_idx = jnp.where(dist >= 0, _t5_bucket(dist), MASKED)
    d = jnp.arange(N_WIN_TILES + 1)[:, None, None]
    dist = KT * d + qq - kj
    win_ok = (dist >= 0) & (dist < NSA_WINDOW) & (d < N_WIN_TILES)
    win_idx = jnp.where(win_ok, _t5_bucket(dist), MASKED)
    cprime = jnp.arange(2 * n_cmp)[:, None] - (n_cmp - QB // CMP_STRIDE)
    cmp_idx = _t5_bucket(qq - CMP_STRIDE * cprime - (CMP_LEN - 1))[None]
    db = jnp.arange(QB)[:, None] + KT - jnp.arange(2 * KT)[None, :]
    swa_idx = jnp.where((db >= 0) & (db < SWA_WINDOW), _t5_bucket(db), MASKED)[None]
    i32 = lambda a: a.astype(jnp.int32)
    return i32(sel_idx), i32(win_idx), i32(cmp_idx), i32(swa_idx)


def _softmax_pv(s_tiles, vt_tiles, ones):
    m = s_tiles[0].max(axis=0, keepdims=True)
    for s in s_tiles[1:]:
        m = jnp.maximum(m, s.max(axis=0, keepdims=True))
    acc = None
    for s, vt in zip(s_tiles, vt_tiles):
        p = jnp.exp(s - m).astype(BF16)
        vx = jnp.concatenate([vt, ones[:, :vt.shape[1]]], axis=0)
        part = _dot(vx, p)
        acc = part if acc is None else acc + part
    return acc[:HEAD_DIM] * (1.0 / acc[HEAD_DIM:HEAD_DIM + 1])


def _nsa_kernel(qt_ref, gt_ref, kc_ref, vct_ref, ks_ref, kw_ref, vst_ref, vwt_ref, et_ref,
                bc_ref, sb_ref, wb_ref, ov_ref, o_ref, kext_scr, m_scr, acc_scr):
    qi = pl.program_id(2)
    t0 = qi * QB
    n_cmp = kc_ref.shape[2]
    n_blk = ks_ref.shape[2] // SEL_BLOCK
    grp = NSA_GROUP
    lanes = grp * QB

    @pl.when(qi == 0)
    def _():
        kext_scr[:, 0:128] = et_ref[...]
        kext_scr[:, 128:128 + HEAD_DIM] = ks_ref[0, 0]
        kext_scr[:, 128 + HEAD_DIM:] = jnp.zeros((kext_scr.shape[0], 128 - HEAD_DIM), BF16)

    qt = qt_ref[0]
    qst = jnp.concatenate([qt[HEAD_DIM * g:HEAD_DIM * (g + 1)] for g in range(grp)], axis=1)
    ones = jnp.ones((16, 2 * KT), BF16)

    start = pl.multiple_of(n_cmp - QB // CMP_STRIDE - (QB // CMP_STRIDE) * qi, 8)
    cr = lax.broadcasted_iota(jnp.int32, (n_cmp, QB), 0)
    cq = lax.broadcasted_iota(jnp.int32, (n_cmp, QB), 1)
    hidden = jnp.where((t0 + cq) >= cr * CMP_STRIDE + CMP_LEN - 1, 0.0, NEG_INF)
    s = _dot(kc_ref[0, 0], qst) + bc_ref[0, 0, pl.ds(start, n_cmp), :]
    s = s + jnp.concatenate([hidden] * grp, axis=1)
    e = jnp.exp(s - s.max(axis=0, keepdims=True))
    tq = t0 + (lax.broadcasted_iota(jnp.int32, (1, lanes), 1) & (QB - 1))
    inv = jnp.where(tq >= CMP_LEN - 1, 1.0 / e.sum(axis=0, keepdims=True), 0.0)
    p = e * inv
    o_c = _dot(vct_ref[0, 0], p.astype(BF16))

    p_sum = p[:, 0:QB]
    for g in range(1, grp):
        p_sum = p_sum + p[:, QB * g:QB * (g + 1)]
    p_hi = p_sum.astype(BF16)
    p_lo = (p_sum - p_hi.astype(F32)).astype(BF16)
    imp = (_dot(ov_ref[...], p_hi) + _dot(ov_ref[...], p_lo))[:n_blk]
    jb = lax.broadcasted_iota(jnp.int32, (n_blk, QB), 0)
    blk_t = (t0 + lax.broadcasted_iota(jnp.int32, (n_blk, QB), 1)) >> SEL_SHIFT
    forced = (jb == 0) | (jb == blk_t) | (jb == blk_t - 1)
    val = jnp.where(forced, imp + FORCE_SCORE, imp)
    val = jnp.where(jb > blk_t, NEG_INF, val)
    n_grp = n_blk // 8
    vals = [val[8 * k:8 * (k + 1)] for k in range(n_grp)]
    ranks = [jnp.zeros((8, QB), jnp.int32) for _ in range(n_grp)]
    row8 = lax.broadcasted_iota(jnp.int32, (8, QB), 0)
    for j in range(n_blk):
        r = jnp.broadcast_to(val[j:j + 1], (8, QB))
        for k in range(n_grp):
            if 8 * k > j:
                beats = r >= vals[k]
            elif 8 * k + 7 <= j:
                beats = r > vals[k]
            else:
                beats = (r > vals[k]) | ((r == vals[k]) & (row8 + 8 * k > j))
            ranks[k] = ranks[k] + jnp.where(beats, 1, 0)
    n_sel = min(N_SEL, n_blk)
    drop = [jnp.where(rk < n_sel, 0.0, NEG_INF) for rk in ranks]
    drop = jnp.concatenate(drop + [jnp.zeros((128 - n_blk, QB), F32)], axis=0).astype(BF16)
    rhs = jnp.concatenate([jnp.concatenate([drop] * grp, axis=1), qst,
                           jnp.zeros((128 - HEAD_DIM, lanes), BF16)], axis=0)

    m_scr[...] = jnp.full(m_scr.shape, NEG_INF, F32)
    acc_scr[...] = jnp.zeros(acc_scr.shape, F32)

    def sel_body(j, carry):
        k0 = pl.multiple_of(j * (2 * KT), 2 * KT)
        s = _dot(kext_scr[pl.ds(k0, 2 * KT), :], rhs)
        d = qi - 2 * j
        e1 = jnp.clip(d, -1, N_NEAR) + 1
        e2 = jnp.clip(d - 1, -1, N_NEAR) + 1
        s = s + jnp.concatenate([sb_ref[0, e1], sb_ref[0, e2]], axis=0)
        m_prev = m_scr[...]
        m_new = jnp.maximum(m_prev, s.max(axis=0, keepdims=True))
        alpha = jnp.exp(m_prev - m_new)
        p = jnp.exp(s - m_new).astype(BF16)
        vx = jnp.concatenate([vst_ref[0, :, pl.ds(k0, 2 * KT)], ones], axis=0)
        acc_scr[...] = acc_scr[...] * alpha + _dot(vx, p)
        m_scr[...] = m_new
        return carry

    lax.fori_loop(0, (qi + 2) // 2, sel_body, 0)
    acc = acc_scr[...]
    o_s = acc[:HEAD_DIM] * (1.0 / acc[HEAD_DIM:HEAD_DIM + 1])

    s_tiles, vt_tiles = [], []
    for d in range(N_WIN_TILES):
        jj = qi - d
        k0 = pl.multiple_of(jnp.maximum(jj, 0) * KT, KT)
        entry = jnp.where(jj >= 0, d, N_WIN_TILES)
        s_tiles.append(_dot(kw_ref[0, 0, pl.ds(k0, KT), :], qst) + wb_ref[0, entry])
        vt_tiles.append(vwt_ref[0, :, pl.ds(k0, KT)])
    o_w = _softmax_pv(s_tiles, vt_tiles, ones)

    gate = jax.nn.sigmoid(gt_ref[0])
    outs = []
    for g in range(grp):
        sl = slice(QB * g, QB * (g + 1))
        outs.append(gate[3 * g:3 * g + 1] * o_c[:, sl] + gate[3 * g + 1:3 * g + 2] * o_s[:, sl]
                    + gate[3 * g + 2:3 * g + 3] * o_w[:, sl])
    o_ref[0] = jnp.concatenate(outs, axis=0).T.astype(o_ref.dtype)


def _nsa_attention(qat, gtt, kc, vct, ks, kw, vst, vwt, et, bct, sbt, wbt, ovt):
    bsz, _, t = qat.shape
    hkv = NSA_KV_HEADS
    n_cmp = kc.shape[2]
    gw = NSA_GROUP * HEAD_DIM
    lanes = NSA_GROUP * QB
    kv_spec = pl.BlockSpec((1, 1, t, HEAD_DIM), lambda bb, hh, i: (bb, hh, 0, 0))
    vt_spec = pl.BlockSpec((1, HEAD_DIM, t), lambda bb, hh, i: (bb, hh, 0))
    tbl = lambda a: pl.BlockSpec((1,) + a.shape[1:], lambda bb, hh, i: (hh, 0, 0, 0))
    const = lambda a: pl.BlockSpec(a.shape, lambda bb, hh, i: (0, 0))
    return pl.pallas_call(
        _nsa_kernel,
        grid=(bsz, hkv, t // QB),
        in_specs=[
            pl.BlockSpec((1, gw, QB), lambda bb, hh, i: (bb, hh, i)),
            pl.BlockSpec((1, GATE_ROWS, QB), lambda bb, hh, i: (bb, hh, i)),
            pl.BlockSpec((1, 1, n_cmp, HEAD_DIM), lambda bb, hh, i: (bb, hh, 0, 0)),
            pl.BlockSpec((1, 1, HEAD_DIM, n_cmp), lambda bb, hh, i: (bb, hh, 0, 0)),
            kv_spec, kv_spec, vt_spec, vt_spec, const(et),
            tbl(bct), tbl(sbt), tbl(wbt), const(ovt),
        ],
        out_specs=pl.BlockSpec((1, QB, gw), lambda bb, hh, i: (bb, i, hh)),
        out_shape=jax.ShapeDtypeStruct((bsz, t, NSA_HEADS * HEAD_DIM), BF16),
        scratch_shapes=[pltpu.VMEM((t, 256), BF16),
                        pltpu.VMEM((1, lanes), F32),
                        pltpu.VMEM((HEAD_DIM + 16, lanes), F32)],
        compiler_params=pltpu.CompilerParams(
            dimension_semantics=("parallel", "parallel", "arbitrary"), vmem_limit_bytes=VMEM_LIMIT),
        name="nsa_attention",
    )(qat, gtt, kc, vct, ks, kw, vst, vwt, et, bct, sbt, wbt, ovt)


def _swa_kernel(sink_ref, qb_ref, kp_ref, kn_ref, vp_ref, vn_ref, wb_ref, o_ref):
    hk = pl.program_id(1)
    qi = pl.program_id(2)
    q = qb_ref[0] * jnp.asarray(HEAD_DIM ** -0.5, BF16)
    k = jnp.concatenate([kp_ref[0, 0], kn_ref[0, 0]], axis=0)
    v = jnp.concatenate([vp_ref[0, 0], vn_ref[0, 0]], axis=0)
    col = lax.broadcasted_iota(jnp.int32, (QB, 2 * KT), 1)
    in_seq = (col >= KT) | (qi > 0)
    outs = []
    for g in range(SWA_GROUP):
        s = _dot_nt(q[:, HEAD_DIM * g:HEAD_DIM * (g + 1)], k) + wb_ref[0, 0, :, 2 * KT * g:2 * KT * (g + 1)]
        s = jnp.where(in_seq, s, NEG_INF)
        sink = sink_ref[hk * SWA_GROUP + g]
        m = jnp.maximum(jnp.max(s, axis=1, keepdims=True), sink)
        e = jnp.exp(s - m)
        p = e * (1.0 / (jnp.sum(e, axis=1, keepdims=True) + jnp.exp(sink - m)))
        outs.append(_dot(p.astype(BF16), v))
    o_ref[0] = jnp.concatenate(outs, axis=1).astype(o_ref.dtype)


def _swa_attention(sinks, qb, kb, vb, wbb):
    bsz, t, _ = qb.shape
    gw = SWA_GROUP * HEAD_DIM
    prev = pl.BlockSpec((1, 1, KT, HEAD_DIM), lambda bb, hh, i: (bb, hh, jnp.maximum(i - 1, 0), 0))
    cur = pl.BlockSpec((1, 1, KT, HEAD_DIM), lambda bb, hh, i: (bb, hh, i, 0))
    return pl.pallas_call(
        _swa_kernel,
        grid=(bsz, SWA_KV_HEADS, t // QB),
        in_specs=[
            pl.BlockSpec(memory_space=pltpu.SMEM),
            pl.BlockSpec((1, QB, gw), lambda bb, hh, i: (bb, i, hh)),
            prev, cur, prev, cur,
            pl.BlockSpec((1,) + wbb.shape[1:], lambda bb, hh, i: (hh, 0, 0, 0)),
        ],
        out_specs=pl.BlockSpec((1, QB, gw), lambda bb, hh, i: (bb, i, hh)),
        out_shape=jax.ShapeDtypeStruct((bsz, t, SWA_HEADS * HEAD_DIM), BF16),
        compiler_params=pltpu.CompilerParams(
            dimension_semantics=("parallel", "parallel", "arbitrary"), vmem_limit_bytes=VMEM_LIMIT),
        name="swa_attention",
    )(sinks, qb, kb, kb, vb, vb, wbb)


def _outproj_kernel(x_ref, oa_ref, ob_ref, w_ref, o_ref):
    na = oa_ref.shape[1]
    o_ref[...] = x_ref[...] + _dot(oa_ref[...], w_ref[:na, :]) + _dot(ob_ref[...], w_ref[na:, :])


def _outproj(x, oa, ob, w):
    n, d = x.shape
    tm = min(ROW_TILE, n)
    row = lambda width: pl.BlockSpec((tm, width), lambda i: (i, 0))
    return pl.pallas_call(
        _outproj_kernel,
        grid=(n // tm,),
        in_specs=[row(d), row(oa.shape[1]), row(ob.shape[1]), pl.BlockSpec(w.shape, lambda i: (0, 0))],
        out_specs=row(d),
        out_shape=jax.ShapeDtypeStruct((n, d), F32),
        compiler_params=pltpu.CompilerParams(
            dimension_semantics=("parallel",), vmem_limit_bytes=VMEM_LIMIT),
        name="outproj",
    )(x, oa, ob, w)


def _ffn_kernel(x_ref, g_ref, wg_ref, wu_ref, wd_ref, gf_ref, o_ref, h_scr, acc_scr, *, final_norm):
    f = pl.program_id(1)

    @pl.when(f == 0)
    def _():
        h_scr[...] = _rms(x_ref[...], g_ref[...]).astype(BF16)
        acc_scr[...] = jnp.zeros(acc_scr.shape, F32)

    h = h_scr[...]
    gate = _dot(h, wg_ref[...])
    act = (gate * jax.nn.sigmoid(gate) * _dot(h, wu_ref[...])).astype(BF16)
    acc_scr[...] += _dot(act, wd_ref[...])

    @pl.when(f == pl.num_programs(1) - 1)
    def _():
        y = x_ref[...] + acc_scr[...]
        o_ref[...] = _rms(y, gf_ref[...]) if final_norm else y


def _ffn(x, g, wg, wu, wd, g_final, final_norm):
    n, d = x.shape
    dff = wg.shape[1]
    tm = min(ROW_TILE, n)
    return pl.pallas_call(
        functools.partial(_ffn_kernel, final_norm=final_norm),
        grid=(n // tm, dff // FF_TILE),
        in_specs=[
            pl.BlockSpec((tm, d), lambda i, f: (i, 0)),
            pl.BlockSpec((1, d), lambda i, f: (0, 0)),
            pl.BlockSpec((d, FF_TILE), lambda i, f: (0, f)),
            pl.BlockSpec((d, FF_TILE), lambda i, f: (0, f)),
            pl.BlockSpec((FF_TILE, d), lambda i, f: (f, 0)),
            pl.BlockSpec((1, d), lambda i, f: (0, 0)),
        ],
        out_specs=pl.BlockSpec((tm, d), lambda i, f: (i, 0)),
        out_shape=jax.ShapeDtypeStruct((n, d), F32),
        scratch_shapes=[pltpu.VMEM((tm, d), BF16), pltpu.VMEM((tm, d), F32)],
        compiler_params=pltpu.CompilerParams(
            dimension_semantics=("parallel", "arbitrary"), vmem_limit_bytes=VMEM_LIMIT),
        name="ffn",
    )(x, g, wg, wu, wd, g_final)


def _cols(w, *ranges):
    return jnp.concatenate([w[..., a:b] for a, b in ranges], axis=-1)


def _split_proj(w_in, b_in):
    wn = _cols(w_in, COL_KC, COL_VC, COL_KS, COL_KW, COL_QB, COL_KB, COL_VB)
    bn = _cols(b_in, COL_KC, COL_VC, COL_KS, COL_KW, COL_QB, COL_KB, COL_VB)

    def gates_padded(a):
        g = _cols(a, COL_GT)
        g = g.reshape(g.shape[:-1] + (NSA_KV_HEADS, 3 * NSA_GROUP))
        pad = [(0, 0)] * (g.ndim - 1) + [(0, GATE_ROWS - 3 * NSA_GROUP)]
        return jnp.pad(g, pad).reshape(g.shape[:-2] + (NSA_KV_HEADS * GATE_ROWS,))

    wt = jnp.concatenate([_cols(w_in, COL_QA, COL_VS, COL_VW), gates_padded(w_in)], axis=-1)
    bt = jnp.concatenate([_cols(b_in, COL_QA, COL_VS, COL_VW), gates_padded(b_in)], axis=-1)
    return wn.astype(BF16), bn.astype(F32), jnp.swapaxes(wt, -1, -2).astype(BF16), bt.astype(F32)


def _overlap_t(n_cmp, n_blk):
    ci = jnp.arange(n_cmp)[None, :] * CMP_STRIDE
    sj = jnp.arange(128)[:, None] * SEL_BLOCK
    ov = (ci < sj + SEL_BLOCK) & (ci + CMP_LEN > sj) & (jnp.arange(128)[:, None] < n_blk)
    return (ov & (jnp.arange(n_cmp)[None, :] < n_cmp - 1)).astype(BF16)


def kernel(x, rel_bias, norm_mix, norm_ffn, w_in, b_in, cmp_pos_k, cmp_pos_v, cmp_k_w1, cmp_k_w2,
           cmp_v_w1, cmp_v_w2, sinks, w_out, w_gate, w_up, w_down, norm_final):
    bsz, t, d = x.shape
    depth = w_in.shape[0]
    n_cmp = t // CMP_STRIDE
    sel_idx, win_idx, cmp_idx, swa_idx = _bias_indices(n_cmp)
    tbl_a, tbl_b = rel_bias[:, :NSA_HEADS], rel_bias[:, NSA_HEADS:]
    sbt = _expand_tables(tbl_a, sel_idx, NSA_GROUP)
    wbt = _expand_tables(tbl_a, win_idx, NSA_GROUP)
    bct = _expand_tables(tbl_a, cmp_idx, NSA_GROUP)
    wbb = _expand_tables(tbl_b, swa_idx, SWA_GROUP)
    ovt = _overlap_t(n_cmp, t // SEL_BLOCK)
    et = (jnp.arange(t)[:, None] // SEL_BLOCK == jnp.arange(128)[None, :]).astype(BF16)
    wn, bn, wt, bt = _split_proj(w_in, b_in)
    for layer in range(depth):
        qat, kc_raw, vc_raw, ks, kw, vst, vwt, gtt, qb, kb, vb = _inproj(
            x, norm_mix[layer][None, :], wn[layer], bn[layer][None, :], wt[layer], bt[layer][:, None])
        kc, vct = _compress(kc_raw, vc_raw, cmp_pos_k[layer], cmp_pos_v[layer],
                            cmp_k_w1[layer].astype(BF16), cmp_k_w2[layer].astype(BF16),
                            cmp_v_w1[layer].astype(BF16), cmp_v_w2[layer].T.astype(BF16))
        o_a = _nsa_attention(qat, gtt, kc, vct, ks, kw, vst, vwt, et, bct, sbt, wbt, ovt)
        o_b = _swa_attention(sinks[layer], qb, kb, vb, wbb)
        x2 = _outproj(x.reshape(bsz * t, d), o_a.reshape(bsz * t, -1), o_b.reshape(bsz * t, -1),
                      w_out[layer].astype(BF16))
        x2 = _ffn(x2, norm_ffn[layer][None, :], w_gate[layer].astype(BF16), w_up[layer].astype(BF16),
                  w_down[layer].astype(BF16), norm_final[None, :], layer == depth - 1)
        x = x2.reshape(bsz, t, d)
    return x
```

```python
import functools
import math

import jax
import jax.numpy as jnp
from jax import lax
from jax.experimental import pallas as pl
from jax.experimental.pallas import tpu as pltpu

F32 = jnp.float32
BF16 = jnp.bfloat16

D_MODEL = 2048
HEAD_DIM = 64
NSA_HEADS = 16
NSA_KV_HEADS = 4
NSA_GROUP = 4
SWA_HEADS = 16
SWA_KV_HEADS = 2
SWA_GROUP = 8
CMP_LEN = 32
CMP_STRIDE = 16
CMP_HIDDEN = 256
SEL_BLOCK = 64
SEL_SHIFT = 6
N_SEL = 16
NSA_WINDOW = 512
SWA_WINDOW = 128
NUM_BUCKETS = 32
REL_MAX_DISTANCE = 1024
RMS_EPS = 1e-5
NEG_INF = -1e30
FORCE_SCORE = 1e6

QB = 128
NQB = 256
KT = 128
Q_PER_K = NQB // KT
C_PER_Q = NQB // CMP_STRIDE
N_NEAR = 8
N_WIN_TILES = NSA_WINDOW // KT + 1
GATE_ROWS = 16
MASKED = NUM_BUCKETS
ROW_TILE = 512
FF_TILE = 512
VMEM_LIMIT = 56 * 1024 * 1024

COL_QA = (0, 1024)
COL_KC, COL_VC = (1024, 1280), (1280, 1536)
COL_KS, COL_VS = (1536, 1792), (1792, 2048)
COL_KW, COL_VW = (2048, 2304), (2304, 2560)
COL_GT = (2560, 2608)
COL_QB = (2608, 3632)
COL_KB, COL_VB = (3632, 3760), (3760, 3888)
N_ROWMAJOR = 4 * 256 + 1024 + 2 * 128
N_COLMAJOR = 1024 + 2 * 256 + NSA_KV_HEADS * GATE_ROWS


def _dot(a, b):
    return jnp.dot(a, b, preferred_element_type=F32)


def _dot_nt(a, b):
    return lax.dot_general(a, b, (((1,), (1,)), ((), ())), preferred_element_type=F32)


def _rms(x, g):
    return x * lax.rsqrt(jnp.mean(x * x, axis=-1, keepdims=True) + RMS_EPS) * g


def _inproj_kernel(x_ref, g_ref, wn_ref, bn_ref, wt_ref, bt_ref,
                   qat_ref, kc_ref, vc_ref, ks_ref, kw_ref, vst_ref, vwt_ref, gtt_ref,
                   qb_ref, kb_ref, vb_ref):
    h = _rms(x_ref[0], g_ref[...]).astype(BF16)

    def rowmajor(a, b):
        return _dot(h, wn_ref[:, a:b]) + bn_ref[:, a:b]

    def colmajor(a, b):
        return _dot_nt(wt_ref[a:b, :], h) + bt_ref[a:b, :]

    for k, ref in enumerate((kc_ref, vc_ref, ks_ref, kw_ref)):
        r = rowmajor(256 * k, 256 * (k + 1))
        for hh in range(NSA_KV_HEADS):
            ref[0, hh] = r[:, HEAD_DIM * hh:HEAD_DIM * (hh + 1)].astype(ref.dtype)
    qb_ref[0] = rowmajor(1024, 2048).astype(qb_ref.dtype)
    for off, ref in ((2048, kb_ref), (2176, vb_ref)):
        r = rowmajor(off, off + 128)
        for hh in range(SWA_KV_HEADS):
            ref[0, hh] = r[:, HEAD_DIM * hh:HEAD_DIM * (hh + 1)].astype(ref.dtype)
    qat_ref[0] = (colmajor(0, 1024) * (HEAD_DIM ** -0.5)).astype(qat_ref.dtype)
    vst_ref[0] = colmajor(1024, 1280).astype(vst_ref.dtype)
    vwt_ref[0] = colmajor(1280, 1536).astype(vwt_ref.dtype)
    gtt_ref[0] = colmajor(1536, N_COLMAJOR)


def _inproj(x, g, wn, bn, wt, bt):
    bsz, t, d = x.shape
    tm = min(ROW_TILE, t)
    kv4 = lambda dt: jax.ShapeDtypeStruct((bsz, NSA_KV_HEADS, t, HEAD_DIM), dt)
    kv2 = jax.ShapeDtypeStruct((bsz, SWA_KV_HEADS, t, HEAD_DIM), BF16)
    colm = lambda rows, dt: jax.ShapeDtypeStruct((bsz, rows, t), dt)
    out_shape = (colm(1024, BF16), kv4(F32), kv4(F32), kv4(BF16), kv4(BF16),
                 colm(256, BF16), colm(256, BF16), colm(NSA_KV_HEADS * GATE_ROWS, F32),
                 jax.ShapeDtypeStruct((bsz, t, 1024), BF16), kv2, kv2)
    kv4_spec = pl.BlockSpec((1, NSA_KV_HEADS, tm, HEAD_DIM), lambda bb, i: (bb, 0, i, 0))
    kv2_spec = pl.BlockSpec((1, SWA_KV_HEADS, tm, HEAD_DIM), lambda bb, i: (bb, 0, i, 0))
    colm_spec = lambda rows: pl.BlockSpec((1, rows, tm), lambda bb, i: (bb, 0, i))
    const = lambda a: pl.BlockSpec(a.shape, lambda bb, i: (0, 0), pipeline_mode=pl.Buffered(1))
    return pl.pallas_call(
        _inproj_kernel,
        grid=(bsz, t // tm),
        in_specs=[pl.BlockSpec((1, tm, d), lambda bb, i: (bb, i, 0)),
                  const(g), const(wn), const(bn), const(wt), const(bt)],
        out_specs=(colm_spec(1024), kv4_spec, kv4_spec, kv4_spec, kv4_spec,
                   colm_spec(256), colm_spec(256), colm_spec(NSA_KV_HEADS * GATE_ROWS),
                   pl.BlockSpec((1, tm, 1024), lambda bb, i: (bb, i, 0)), kv2_spec, kv2_spec),
        out_shape=out_shape,
        compiler_params=pltpu.CompilerParams(
            dimension_semantics=("parallel", "parallel"), vmem_limit_bytes=VMEM_LIMIT),
        name="inproj",
    )(x, g, wn, bn, wt, bt)


def _compress_kernel(kr_ref, vr_ref, pk_ref, pv_ref, k1_ref, k2_ref, v1_ref, v2t_ref,
                     ko_ref, vot_ref):
    n_chunk = kr_ref.shape[2] // CMP_STRIDE
    half = CMP_STRIDE * HEAD_DIM

    def hidden(r_ref, p_ref, w1_ref):
        top = jnp.zeros((n_chunk, CMP_HIDDEN), F32)
        bot = jnp.zeros((n_chunk, CMP_HIDDEN), F32)
        for l in range(CMP_STRIDE):
            xl = r_ref[0, 0, pl.ds(l, n_chunk, stride=CMP_STRIDE), :]
            a = (xl + p_ref[l:l + 1, :]).astype(BF16)
            top = top + _dot(a, w1_ref[HEAD_DIM * l:HEAD_DIM * (l + 1), :])
            c = (xl + p_ref[CMP_STRIDE + l:CMP_STRIDE + l + 1, :]).astype(BF16)
            bot = bot + _dot(c, w1_ref[half + HEAD_DIM * l:half + HEAD_DIM * (l + 1), :])
        hid = top + pltpu.roll(bot, n_chunk - 1, axis=0)
        return jax.nn.gelu(hid).astype(BF16)

    out = _dot(hidden(kr_ref, pk_ref, k1_ref), k2_ref[...])
    row = lax.broadcasted_iota(jnp.int32, out.shape, 0)
    ko_ref[0, 0] = jnp.where(row < n_chunk - 1, out, 0.0).astype(ko_ref.dtype)
    out_t = _dot_nt(v2t_ref[...], hidden(vr_ref, pv_ref, v1_ref))
    col = lax.broadcasted_iota(jnp.int32, out_t.shape, 1)
    vot_ref[0, 0] = jnp.where(col < n_chunk - 1, out_t, 0.0).astype(vot_ref.dtype)


def _compress(kc_raw, vc_raw, pos_k, pos_v, k1, k2, v1, v2t):
    bsz, hkv, t, dh = kc_raw.shape
    n_chunk = t // CMP_STRIDE
    raw_spec = pl.BlockSpec((1, 1, t, dh), lambda bb, hh: (bb, hh, 0, 0))
    full = lambda a: pl.BlockSpec(a.shape, lambda bb, hh: (0,) * a.ndim)
    return pl.pallas_call(
        _compress_kernel,
        grid=(bsz, hkv),
        in_specs=[raw_spec, raw_spec, full(pos_k), full(pos_v), full(k1), full(k2), full(v1), full(v2t)],
        out_specs=(pl.BlockSpec((1, 1, n_chunk, dh), lambda bb, hh: (bb, hh, 0, 0)),
                   pl.BlockSpec((1, 1, dh, n_chunk), lambda bb, hh: (bb, hh, 0, 0))),
        out_shape=(jax.ShapeDtypeStruct((bsz, hkv, n_chunk, dh), BF16),
                   jax.ShapeDtypeStruct((bsz, hkv, dh, n_chunk), BF16)),
        compiler_params=pltpu.CompilerParams(
            dimension_semantics=("parallel", "parallel"), vmem_limit_bytes=VMEM_LIMIT),
        name="compress",
    )(kc_raw, vc_raw, pos_k, pos_v, k1, k2, v1, v2t)


def _expand_kernel(tbl_ref, idx_ref, o_ref, *, gsz):
    grp = pl.program_id(0)
    idx = idx_ref[0]
    w = idx.shape[1]
    accs = [jnp.full(idx.shape, tbl_ref[MASKED, grp * gsz + g], F32) for g in range(gsz)]
    for b in range(NUM_BUCKETS):
        hit = idx == b
        for g in range(gsz):
            accs[g] = jnp.where(hit, tbl_ref[b, grp * gsz + g], accs[g])
    for g in range(gsz):
        o_ref[0, 0, :, w * g:w * (g + 1)] = accs[g]


def _expand_tables(tbl, idx, gsz):
    r0, r1, w = idx.shape
    heads = tbl.shape[1]
    tr = min(r1, 128)
    tbl_ext = jnp.concatenate([tbl.astype(F32), jnp.full((1, heads), NEG_INF, F32)], axis=0)
    return pl.pallas_call(
        functools.partial(_expand_kernel, gsz=gsz),
        grid=(heads // gsz, r0, r1 // tr),
        in_specs=[pl.BlockSpec(memory_space=pltpu.SMEM),
                  pl.BlockSpec((1, tr, w), lambda gg, a, r: (a, r, 0))],
        out_specs=pl.BlockSpec((1, 1, tr, gsz * w), lambda gg, a, r: (gg, a, r, 0)),
        out_shape=jax.ShapeDtypeStruct((heads // gsz, r0, r1, gsz * w), F32),
        compiler_params=pltpu.CompilerParams(
            dimension_semantics=("parallel", "parallel", "parallel"), vmem_limit_bytes=VMEM_LIMIT),
        name="expand_bias",
    )(tbl_ext, idx)


def _t5_bucket(dist):
    dist = jnp.maximum(dist, 0)
    max_exact = NUM_BUCKETS // 2
    d = jnp.maximum(dist, 1).astype(F32)
    ratio = jnp.log(d / max_exact) / math.log(REL_MAX_DISTANCE / max_exact)
    large = max_exact + (ratio * (NUM_BUCKETS - max_exact)).astype(jnp.int32)
    large = jnp.minimum(large, NUM_BUCKETS - 1)
    return jnp.where(dist < max_exact, dist, large)


def _bias_indices(n_cmp):
    kj = jnp.arange(KT)[:, None]
    qq = jnp.arange(NQB)[None, :]
    d = jnp.arange(-Q_PER_K, N_NEAR + 1)[:, None, None]
    dist = KT * d + qq - kj
    sel_idx = jnp.where(dist >= 0, _t5_bucket(dist), MASKED)
    d = jnp.arange(-(Q_PER_K - 1), N_WIN_TILES + 1)[:, None, None]
    dist = KT * d + qq - kj
    win_idx = jnp.where((dist >= 0) & (dist < NSA_WINDOW), _t5_bucket(dist), MASKED)
    cprime = jnp.arange(2 * n_cmp)[:, None] - (n_cmp - C_PER_Q)
    cmp_idx = _t5_bucket(qq - CMP_STRIDE * cprime - (CMP_LEN - 1))[None]
    db = jnp.arange(QB)[:, None] + KT - jnp.arange(2 * KT)[None, :]
    swa_idx = jnp.where((db >= 0) & (db < SWA_WINDOW), _t5_bucket(db), MASKED)[None]
    i32 = lambda a: a.astype(jnp.int32)
    return i32(sel_idx), i32(win_idx), i32(cmp_idx), i32(swa_idx)


def _softmax_pv(s_tiles, vt_tiles, ones):
    m = s_tiles[0].max(axis=0, keepdims=True)
    for s in s_tiles[1:]:
        m = jnp.maximum(m, s.max(axis=0, keepdims=True))
    acc = None
    for s, vt in zip(s_tiles, vt_tiles):
        p = jnp.exp(s - m).astype(BF16)
        vx = jnp.concatenate([vt, ones[:, :vt.shape[1]]], axis=0)
        part = _dot(vx, p)
        acc = part if acc is None else acc + part
    return acc[:HEAD_DIM] * (1.0 / acc[HEAD_DIM:HEAD_DIM + 1])


def _nsa_kernel(qt_ref, gt_ref, kc_ref, vct_ref, ks_ref, kw_ref, vst_ref, vwt_ref, et_ref,
                bc_ref, sb_ref, wb_ref, ov_ref, o_ref,
                kext_scr, rhs_scr, s_scr, mt_scr, p_scr, al_scr, m_scr, acc_scr):
    qi = pl.program_id(2)
    t0 = qi * NQB
    n_cmp = kc_ref.shape[2]
    n_blk = ks_ref.shape[2] // SEL_BLOCK
    grp = NSA_GROUP
    lanes = grp * NQB

    @pl.when(qi == 0)
    def _():
        kext_scr[:, 0:128] = et_ref[...]
        kext_scr[:, 128:128 + HEAD_DIM] = ks_ref[0, 0]
        kext_scr[:, 128 + HEAD_DIM:] = jnp.zeros((kext_scr.shape[0], 128 - HEAD_DIM), BF16)

    qt = qt_ref[0]
    qst = jnp.concatenate([qt[HEAD_DIM * g:HEAD_DIM * (g + 1)] for g in range(grp)], axis=1)
    ones = jnp.ones((16, 2 * KT), BF16)

    start = pl.multiple_of(n_cmp - C_PER_Q - C_PER_Q * qi, 8)
    cr = lax.broadcasted_iota(jnp.int32, (n_cmp, NQB), 0)
    cq = lax.broadcasted_iota(jnp.int32, (n_cmp, NQB), 1)
    hidden = jnp.where((t0 + cq) >= cr * CMP_STRIDE + CMP_LEN - 1, 0.0, NEG_INF)
    s = _dot(kc_ref[0, 0], qst) + bc_ref[0, 0, pl.ds(start, n_cmp), :]
    s = s + jnp.concatenate([hidden] * grp, axis=1)
    e = jnp.exp(s - s.max(axis=0, keepdims=True))
    tq = t0 + (lax.broadcasted_iota(jnp.int32, (1, lanes), 1) & (NQB - 1))
    inv = jnp.where(tq >= CMP_LEN - 1, 1.0 / e.sum(axis=0, keepdims=True), 0.0)
    p = e * inv
    o_c = _dot(vct_ref[0, 0], p.astype(BF16))

    s_tiles, vt_tiles = [], []
    n_win_entries = Q_PER_K - 1 + N_WIN_TILES
    for d in range(-(Q_PER_K - 1), N_WIN_TILES):
        jj = Q_PER_K * qi - d
        k0 = pl.multiple_of(jnp.maximum(jj, 0) * KT, KT)
        entry = jnp.where(jj >= 0, d + Q_PER_K - 1, n_win_entries)
        s_tiles.append(_dot(kw_ref[0, 0, pl.ds(k0, KT), :], qst) + wb_ref[0, entry])
        vt_tiles.append(vwt_ref[0, :, pl.ds(k0, KT)])
    o_w = _softmax_pv(s_tiles, vt_tiles, ones)

    p_sum = p[:, 0:NQB]
    for g in range(1, grp):
        p_sum = p_sum + p[:, NQB * g:NQB * (g + 1)]
    p_hi = p_sum.astype(BF16)
    p_lo = (p_sum - p_hi.astype(F32)).astype(BF16)
    imp = (_dot(ov_ref[...], p_hi) + _dot(ov_ref[...], p_lo))[:n_blk]
    jb = lax.broadcasted_iota(jnp.int32, (n_blk, NQB), 0)
    blk_t = (t0 + lax.broadcasted_iota(jnp.int32, (n_blk, NQB), 1)) >> SEL_SHIFT
    forced = (jb == 0) | (jb == blk_t) | (jb == blk_t - 1)
    val = jnp.where(forced, imp + FORCE_SCORE, imp)
    val = jnp.where(jb > blk_t, NEG_INF, val)
    n_grp = n_blk // 8
    vals = [val[8 * k:8 * (k + 1)] for k in range(n_grp)]
    ranks = [jnp.zeros((8, NQB), jnp.int32) for _ in range(n_grp)]
    row8 = lax.broadcasted_iota(jnp.int32, (8, NQB), 0)
    for j in range(n_blk):
        r = jnp.broadcast_to(val[j:j + 1], (8, NQB))
        for k in range(n_grp):
            if 8 * k > j:
                beats = r >= vals[k]
            elif 8 * k + 7 <= j:
                beats = r > vals[k]
            else:
                beats = (r > vals[k]) | ((r == vals[k]) & (row8 + 8 * k > j))
            ranks[k] = ranks[k] + jnp.where(beats, 1, 0)
    n_sel = min(N_SEL, n_blk)
    drop = [jnp.where(rk < n_sel, 0.0, NEG_INF) for rk in ranks]
    drop = jnp.concatenate(drop + [jnp.zeros((128 - n_blk, NQB), F32)], axis=0).astype(BF16)
    rhs = jnp.concatenate([jnp.concatenate([drop] * grp, axis=1), qst,
                           jnp.zeros((128 - HEAD_DIM, lanes), BF16)], axis=0)

    rhs_scr[...] = rhs

    tk = 2 * KT
    last_tile = kext_scr.shape[0] // tk - 1

    def logits(i):
        k0 = pl.multiple_of(jnp.minimum(i, last_tile) * tk, tk)
        s = _dot(kext_scr[pl.ds(k0, tk), :], rhs_scr[...])
        d = Q_PER_K * qi - 2 * i
        e1 = jnp.clip(d, -Q_PER_K, N_NEAR) + Q_PER_K
        e2 = jnp.clip(d - 1, -Q_PER_K, N_NEAR) + Q_PER_K
        s = s + jnp.concatenate([sb_ref[0, e1], sb_ref[0, e2]], axis=0)
        return s, s.max(axis=0, keepdims=True)

    def softmax(s, mt, m_prev):
        m_new = jnp.maximum(m_prev, mt)
        return jnp.exp(s - m_new).astype(BF16), jnp.exp(m_prev - m_new), m_new

    def pv(i, p, alpha, acc):
        k0 = pl.multiple_of(jnp.clip(i, 0, last_tile) * tk, tk)
        vx = jnp.concatenate([vst_ref[0, :, pl.ds(k0, tk)], ones], axis=0)
        return acc * alpha + _dot(vx, p)

    s0, mt0 = logits(0)
    s1, mt1 = logits(1)
    p0, a0, m0 = softmax(s0, mt0, jnp.full((1, lanes), NEG_INF, F32))
    s_scr[1], mt_scr[1] = s1, mt1
    p_scr[0], al_scr[0] = p0, a0
    m_scr[...] = m0
    acc_scr[...] = jnp.zeros(acc_scr.shape, F32)

    def sel_body(i, carry):
        par = i & 1
        p_old, a_old, acc = p_scr[par], al_scr[par], acc_scr[...]
        s_cur, mt_cur, m_prev = s_scr[1 - par], mt_scr[1 - par], m_scr[...]
        s_new, mt_new = logits(i)
        acc_scr[...] = pv(i - 2, p_old, a_old, acc)
        p_cur, a_cur, m_new = softmax(s_cur, mt_cur, m_prev)
        p_scr[1 - par], al_scr[1 - par] = p_cur, a_cur
        m_scr[...] = m_new
        s_scr[par], mt_scr[par] = s_new, mt_new
        return carry

    n_tiles = jnp.maximum((Q_PER_K * (qi + 1) + 1) // 2, 2)
    lax.fori_loop(2, n_tiles, sel_body, 0)
    par = n_tiles & 1
    acc = pv(n_tiles - 2, p_scr[par], al_scr[par], acc_scr[...])
    p_last, a_last, _ = softmax(s_scr[1 - par], mt_scr[1 - par], m_scr[...])
    acc = pv(n_tiles - 1, p_last, a_last, acc)
    o_s = acc[:HEAD_DIM] * (1.0 / acc[HEAD_DIM:HEAD_DIM + 1])

    gate = jax.nn.sigmoid(gt_ref[0])
    outs = []
    for g in range(grp):
        sl = slice(NQB * g, NQB * (g + 1))
        outs.append(gate[3 * g:3 * g + 1] * o_c[:, sl] + gate[3 * g + 1:3 * g + 2] * o_s[:, sl]
                    + gate[3 * g + 2:3 * g + 3] * o_w[:, sl])
    o_ref[0] = jnp.concatenate(outs, axis=0).T.astype(o_ref.dtype)


def _nsa_attention(qat, gtt, kc, vct, ks, kw, vst, vwt, et, bct, sbt, wbt, ovt):
    bsz, _, t = qat.shape
    hkv = NSA_KV_HEADS
    n_cmp = kc.shape[2]
    gw = NSA_GROUP * HEAD_DIM
    lanes = NSA_GROUP * NQB
    kv_spec = pl.BlockSpec((1, 1, t, HEAD_DIM), lambda bb, hh, i: (bb, hh, 0, 0))
    vt_spec = pl.BlockSpec((1, HEAD_DIM, t), lambda bb, hh, i: (bb, hh, 0))
    tbl = lambda a: pl.BlockSpec((1,) + a.shape[1:], lambda bb, hh, i: (hh, 0, 0, 0))
    const = lambda a: pl.BlockSpec(a.shape, lambda bb, hh, i: (0, 0))
    return pl.pallas_call(
        _nsa_kernel,
        grid=(bsz, hkv, t // NQB),
        in_specs=[
            pl.BlockSpec((1, gw, NQB), lambda bb, hh, i: (bb, hh, i)),
            pl.BlockSpec((1, GATE_ROWS, NQB), lambda bb, hh, i: (bb, hh, i)),
            pl.BlockSpec((1, 1, n_cmp, HEAD_DIM), lambda bb, hh, i: (bb, hh, 0, 0)),
            pl.BlockSpec((1, 1, HEAD_DIM, n_cmp), lambda bb, hh, i: (bb, hh, 0, 0)),
            kv_spec, kv_spec, vt_spec, vt_spec, const(et),
            tbl(bct), tbl(sbt), tbl(wbt), const(ovt),
        ],
        out_specs=pl.BlockSpec((1, NQB, gw), lambda bb, hh, i: (bb, i, hh)),
        out_shape=jax.ShapeDtypeStruct((bsz, t, NSA_HEADS * HEAD_DIM), BF16),
        scratch_shapes=[pltpu.VMEM((t, 256), BF16),
                        pltpu.VMEM((256, lanes), BF16),
                        pltpu.VMEM((2, 2 * KT, lanes), F32),
                        pltpu.VMEM((2, 1, lanes), F32),
                        pltpu.VMEM((2, 2 * KT, lanes), BF16),
                        pltpu.VMEM((2, 1, lanes), F32),
                        pltpu.VMEM((1, lanes), F32),
                        pltpu.VMEM((HEAD_DIM + 16, lanes), F32)],
        compiler_params=pltpu.CompilerParams(
            dimension_semantics=("parallel", "parallel", "arbitrary"), vmem_limit_bytes=VMEM_LIMIT),
        name="nsa_attention",
    )(qat, gtt, kc, vct, ks, kw, vst, vwt, et, bct, sbt, wbt, ovt)


def _swa_kernel(sink_ref, qb_ref, kp_ref, kn_ref, vp_ref, vn_ref, wb_ref, o_ref):
    hk = pl.program_id(1)
    qi = pl.program_id(2)
    q = qb_ref[0] * jnp.asarray(HEAD_DIM ** -0.5, BF16)
    k = jnp.concatenate([kp_ref[0, 0], kn_ref[0, 0]], axis=0)
    v = jnp.concatenate([vp_ref[0, 0], vn_ref[0, 0]], axis=0)
    col = lax.broadcasted_iota(jnp.int32, (QB, 2 * KT), 1)
    in_seq = (col >= KT) | (qi > 0)
    outs = []
    for g in range(SWA_GROUP):
        s = _dot_nt(q[:, HEAD_DIM * g:HEAD_DIM * (g + 1)], k) + wb_ref[0, 0, :, 2 * KT * g:2 * KT * (g + 1)]
        s = jnp.where(in_seq, s, NEG_INF)
        sink = sink_ref[hk * SWA_GROUP + g]
        m = jnp.maximum(jnp.max(s, axis=1, keepdims=True), sink)
        e = jnp.exp(s - m)
        p = e * (1.0 / (jnp.sum(e, axis=1, keepdims=True) + jnp.exp(sink - m)))
        outs.append(_dot(p.astype(BF16), v))
    o_ref[0] = jnp.concatenate(outs, axis=1).astype(o_ref.dtype)


def _swa_attention(sinks, qb, kb, vb, wbb):
    bsz, t, _ = qb.shape
    gw = SWA_GROUP * HEAD_DIM
    prev = pl.BlockSpec((1, 1, KT, HEAD_DIM), lambda bb, hh, i: (bb, hh, jnp.maximum(i - 1, 0), 0))
    cur = pl.BlockSpec((1, 1, KT, HEAD_DIM), lambda bb, hh, i: (bb, hh, i, 0))
    return pl.pallas_call(
        _swa_kernel,
        grid=(bsz, SWA_KV_HEADS, t // QB),
        in_specs=[
            pl.BlockSpec(memory_space=pltpu.SMEM),
            pl.BlockSpec((1, QB, gw), lambda bb, hh, i: (bb, i, hh)),
            prev, cur, prev, cur,
            pl.BlockSpec((1,) + wbb.shape[1:], lambda bb, hh, i: (hh, 0, 0, 0)),
        ],
        out_specs=pl.BlockSpec((1, QB, gw), lambda bb, hh, i: (bb, i, hh)),
        out_shape=jax.ShapeDtypeStruct((bsz, t, SWA_HEADS * HEAD_DIM), BF16),
        compiler_params=pltpu.CompilerParams(
            dimension_semantics=("parallel", "parallel", "arbitrary"), vmem_limit_bytes=VMEM_LIMIT),
        name="swa_attention",
    )(sinks, qb, kb, kb, vb, vb, wbb)


def _outproj_kernel(x_ref, oa_ref, ob_ref, w_ref, o_ref):
    na = oa_ref.shape[1]
    o_ref[...] = x_ref[...] + _dot(oa_ref[...], w_ref[:na, :]) + _dot(ob_ref[...], w_ref[na:, :])


def _outproj(x, oa, ob, w):
    n, d = x.shape
    tm = min(ROW_TILE, n)
    row = lambda width: pl.BlockSpec((tm, width), lambda i: (i, 0))
    return pl.pallas_call(
        _outproj_kernel,
        grid=(n // tm,),
        in_specs=[row(d), row(oa.shape[1]), row(ob.shape[1]), pl.BlockSpec(w.shape, lambda i: (0, 0))],
        out_specs=row(d),
        out_shape=jax.ShapeDtypeStruct((n, d), F32),
        compiler_params=pltpu.CompilerParams(
            dimension_semantics=("parallel",), vmem_limit_bytes=VMEM_LIMIT),
        name="outproj",
    )(x, oa, ob, w)


def _ffn_kernel(x_ref, g_ref, wg_ref, wu_ref, wd_ref, gf_ref, o_ref, h_scr, acc_scr, *, final_norm):
    f = pl.program_id(1)

    @pl.when(f == 0)
    def _():
        h_scr[...] = _rms(x_ref[...], g_ref[...]).astype(BF16)
        acc_scr[...] = jnp.zeros(acc_scr.shape, F32)

    h = h_scr[...]
    gate = _dot(h, wg_ref[...])
    act = (gate * jax.nn.sigmoid(gate) * _dot(h, wu_ref[...])).astype(BF16)
    acc_scr[...] += _dot(act, wd_ref[...])

    @pl.when(f == pl.num_programs(1) - 1)
    def _():
        y = x_ref[...] + acc_scr[...]
        o_ref[...] = _rms(y, gf_ref[...]) if final_norm else y


def _ffn(x, g, wg, wu, wd, g_final, final_norm):
    n, d = x.shape
    dff = wg.shape[1]
    tm = min(ROW_TILE, n)
    return pl.pallas_call(
        functools.partial(_ffn_kernel, final_norm=final_norm),
        grid=(n // tm, dff // FF_TILE),
        in_specs=[
            pl.BlockSpec((tm, d), lambda i, f: (i, 0)),
            pl.BlockSpec((1, d), lambda i, f: (0, 0)),
            pl.BlockSpec((d, FF_TILE), lambda i, f: (0, f)),
            pl.BlockSpec((d, FF_TILE), lambda i, f: (0, f)),
            pl.BlockSpec((FF_TILE, d), lambda i, f: (f, 0)),
            pl.BlockSpec((1, d), lambda i, f: (0, 0)),
        ],
        out_specs=pl.BlockSpec((tm, d), lambda i, f: (i, 0)),
        out_shape=jax.ShapeDtypeStruct((n, d), F32),
        scratch_shapes=[pltpu.VMEM((tm, d), BF16), pltpu.VMEM((tm, d), F32)],
        compiler_params=pltpu.CompilerParams(
            dimension_semantics=("parallel", "arbitrary"), vmem_limit_bytes=VMEM_LIMIT),
        name="ffn",
    )(x, g, wg, wu, wd, g_final)


def _cols(w, *ranges):
    return jnp.concatenate([w[..., a:b] for a, b in ranges], axis=-1)


def _split_proj(w_in, b_in):
    w_in = w_in.astype(BF16)
    wn = _cols(w_in, COL_KC, COL_VC, COL_KS, COL_KW, COL_QB, COL_KB, COL_VB)
    bn = _cols(b_in, COL_KC, COL_VC, COL_KS, COL_KW, COL_QB, COL_KB, COL_VB)

    def gates_padded(a):
        g = _cols(a, COL_GT)
        g = g.reshape(g.shape[:-1] + (NSA_KV_HEADS, 3 * NSA_GROUP))
        pad = [(0, 0)] * (g.ndim - 1) + [(0, GATE_ROWS - 3 * NSA_GROUP)]
        return jnp.pad(g, pad).reshape(g.shape[:-2] + (NSA_KV_HEADS * GATE_ROWS,))

    wt = jnp.concatenate([_cols(w_in, COL_QA, COL_VS, COL_VW), gates_padded(w_in)], axis=-1)
    bt = jnp.concatenate([_cols(b_in, COL_QA, COL_VS, COL_VW), gates_padded(b_in)], axis=-1)
    return wn.astype(BF16), bn.astype(F32), jnp.swapaxes(wt, -1, -2).astype(BF16), bt.astype(F32)


def _overlap_t(n_cmp, n_blk):
    ci = jnp.arange(n_cmp)[None, :] * CMP_STRIDE
    sj = jnp.arange(128)[:, None] * SEL_BLOCK
    ov = (ci < sj + SEL_BLOCK) & (ci + CMP_LEN > sj) & (jnp.arange(128)[:, None] < n_blk)
    return (ov & (jnp.arange(n_cmp)[None, :] < n_cmp - 1)).astype(BF16)


def kernel(x, rel_bias, norm_mix, norm_ffn, w_in, b_in, cmp_pos_k, cmp_pos_v, cmp_k_w1, cmp_k_w2,
           cmp_v_w1, cmp_v_w2, sinks, w_out, w_gate, w_up, w_down, norm_final):
    bsz, t, d = x.shape
    depth = w_in.shape[0]
    n_cmp = t // CMP_STRIDE
    sel_idx, win_idx, cmp_idx, swa_idx = _bias_indices(n_cmp)
    tbl_a, tbl_b = rel_bias[:, :NSA_HEADS], rel_bias[:, NSA_HEADS:]
    sbt = _expand_tables(tbl_a, sel_idx, NSA_GROUP)
    wbt = _expand_tables(tbl_a, win_idx, NSA_GROUP)
    bct = _expand_tables(tbl_a, cmp_idx, NSA_GROUP)
    wbb = _expand_tables(tbl_b, swa_idx, SWA_GROUP)
    ovt = _overlap_t(n_cmp, t // SEL_BLOCK)
    et = (jnp.arange(t)[:, None] // SEL_BLOCK == jnp.arange(128)[None, :]).astype(BF16)
    wn, bn, wt, bt = _split_proj(w_in, b_in)
    for layer in range(depth):
        qat, kc_raw, vc_raw, ks, kw, vst, vwt, gtt, qb, kb, vb = _inproj(
            x, norm_mix[layer][None, :], wn[layer], bn[layer][None, :], wt[layer], bt[layer][:, None])
        kc, vct = _compress(kc_raw, vc_raw, cmp_pos_k[layer], cmp_pos_v[layer],
                            cmp_k_w1[layer].astype(BF16), cmp_k_w2[layer].astype(BF16),
                            cmp_v_w1[layer].astype(BF16), cmp_v_w2[layer].T.astype(BF16))
        o_a = _nsa_attention(qat, gtt, kc, vct, ks, kw, vst, vwt, et, bct, sbt, wbt, ovt)
        o_b = _swa_attention(sinks[layer], qb, kb, vb, wbb)
        x2 = _outproj(x.reshape(bsz * t, d), o_a.reshape(bsz * t, -1), o_b.reshape(bsz * t, -1),
                      w_out[layer].astype(BF16))
        x2 = _ffn(x2, norm_ffn[layer][None, :], w_gate[layer].astype(BF16), w_up[layer].astype(BF16),
                  w_down[layer].astype(BF16), norm_final[None, :], layer == depth - 1)
        x = x2.reshape(bsz, t, d)
    return x
```

```python
import functools
import math

import jax
import jax.numpy as jnp
from jax import lax
from jax.experimental import pallas as pl
from jax.experimental.pallas import tpu as pltpu

F32 = jnp.float32
BF16 = jnp.bfloat16

D_MODEL = 2048
HEAD_DIM = 64
NSA_HEADS = 16
NSA_KV_HEADS = 4
NSA_GROUP = 4
SWA_HEADS = 16
SWA_KV_HEADS = 2
SWA_GROUP = 8
CMP_LEN = 32
CMP_STRIDE = 16
CMP_HIDDEN = 256
SEL_BLOCK = 64
SEL_SHIFT = 6
N_SEL = 16
NSA_WINDOW = 512
SWA_WINDOW = 128
NUM_BUCKETS = 32
REL_MAX_DISTANCE = 1024
RMS_EPS = 1e-5
NEG_INF = -1e30
FORCE_SCORE = 1e6

QB = 128
NQB = 256
KT = 128
Q_PER_K = NQB // KT
C_PER_Q = NQB // CMP_STRIDE
N_NEAR = 8
N_WIN_TILES = NSA_WINDOW // KT + 1
MASKED = NUM_BUCKETS
ROW_TILE = 512
FF_TILE = 512
VMEM_LIMIT = 56 * 1024 * 1024

COL_QA = (0, 1024)
COL_KC, COL_VC = (1024, 1280), (1280, 1536)
COL_KS, COL_VS = (1536, 1792), (1792, 2048)
COL_KW, COL_VW = (2048, 2304), (2304, 2560)
COL_GT = (2560, 2608)
COL_QB = (2608, 3632)
COL_KB, COL_VB = (3632, 3760), (3760, 3888)
GATE_ROWS = 16
N_GATES = NSA_KV_HEADS * GATE_ROWS
ROWMAJOR = (COL_KC, COL_VC, COL_KS, COL_KW, COL_KB)


def _dot(a, b):
    return jnp.dot(a, b, preferred_element_type=F32)


def _dot_nt(a, b):
    return lax.dot_general(a, b, (((1,), (1,)), ((), ())), preferred_element_type=F32)


def _rms(x, g):
    return x * lax.rsqrt(jnp.mean(x * x, axis=-1, keepdims=True) + RMS_EPS) * g


def _inproj_kernel(x_ref, g_ref, wt_ref, br_ref, bc_ref, wg_ref, bg_ref,
                   qat_ref, kc_ref, vc_ref, ks_ref, kw_ref, vst_ref, vwt_ref, gtt_ref,
                   qbt_ref, kb_ref, vbt_ref):
    h = _rms(x_ref[0], g_ref[...]).astype(BF16)

    def colmajor(cols):
        a, b = cols
        return _dot_nt(wt_ref[a:b, :], h) + bc_ref[a:b, :]

    off = 0
    for (a, b), ref in zip(ROWMAJOR, (kc_ref, vc_ref, ks_ref, kw_ref, kb_ref)):
        r = _dot_nt(h, wt_ref[a:b, :]) + br_ref[:, off:off + b - a]
        off += b - a
        for hh in range((b - a) // HEAD_DIM):
            ref[0, hh] = r[:, HEAD_DIM * hh:HEAD_DIM * (hh + 1)].astype(ref.dtype)
    qat_ref[0] = (colmajor(COL_QA) * (HEAD_DIM ** -0.5)).astype(qat_ref.dtype)
    qbt_ref[0] = (colmajor(COL_QB) * (HEAD_DIM ** -0.5)).astype(qbt_ref.dtype)
    vst_ref[0] = colmajor(COL_VS).astype(vst_ref.dtype)
    vwt_ref[0] = colmajor(COL_VW).astype(vwt_ref.dtype)
    vbt_ref[0] = colmajor(COL_VB).astype(vbt_ref.dtype)
    gtt_ref[0] = _dot_nt(wg_ref[...], h) + bg_ref[...]


def _inproj(x, g, wt, b_row, b_col, wg, bg):
    bsz, t, d = x.shape
    tm = min(ROW_TILE, t)
    kv4 = lambda dt: jax.ShapeDtypeStruct((bsz, NSA_KV_HEADS, t, HEAD_DIM), dt)
    kv2 = jax.ShapeDtypeStruct((bsz, SWA_KV_HEADS, t, HEAD_DIM), BF16)
    colm = lambda rows, dt: jax.ShapeDtypeStruct((bsz, rows, t), dt)
    out_shape = (colm(1024, BF16), kv4(F32), kv4(F32), kv4(BF16), kv4(BF16),
                 colm(256, BF16), colm(256, BF16), colm(N_GATES, F32),
                 colm(1024, BF16), kv2, colm(128, BF16))
    kv4_spec = pl.BlockSpec((1, NSA_KV_HEADS, tm, HEAD_DIM), lambda bb, i: (bb, 0, i, 0))
    kv2_spec = pl.BlockSpec((1, SWA_KV_HEADS, tm, HEAD_DIM), lambda bb, i: (bb, 0, i, 0))
    colm_spec = lambda rows: pl.BlockSpec((1, rows, tm), lambda bb, i: (bb, 0, i))
    const = lambda a: pl.BlockSpec(a.shape, lambda bb, i: (0, 0), pipeline_mode=pl.Buffered(1))
    return pl.pallas_call(
        _inproj_kernel,
        grid=(bsz, t // tm),
        in_specs=[pl.BlockSpec((1, tm, d), lambda bb, i: (bb, i, 0)),
                  const(g), const(wt), const(b_row), const(b_col), const(wg), const(bg)],
        out_specs=(colm_spec(1024), kv4_spec, kv4_spec, kv4_spec, kv4_spec,
                   colm_spec(256), colm_spec(256), colm_spec(N_GATES),
                   colm_spec(1024), kv2_spec, colm_spec(128)),
        out_shape=out_shape,
        compiler_params=pltpu.CompilerParams(
            dimension_semantics=("parallel", "parallel"), vmem_limit_bytes=VMEM_LIMIT),
        name="inproj",
    )(x, g, wt, b_row, b_col, wg, bg)


def _compress_kernel(kr_ref, vr_ref, pk_ref, pv_ref, k1_ref, k2_ref, v1_ref, v2t_ref,
                     ko_ref, vot_ref):
    n_chunk = kr_ref.shape[2] // CMP_STRIDE
    half = CMP_STRIDE * HEAD_DIM

    def hidden(r_ref, p_ref, w1_ref):
        top = jnp.zeros((n_chunk, CMP_HIDDEN), F32)
        bot = jnp.zeros((n_chunk, CMP_HIDDEN), F32)
        for l in range(CMP_STRIDE):
            xl = r_ref[0, 0, pl.ds(l, n_chunk, stride=CMP_STRIDE), :]
            a = (xl + p_ref[l:l + 1, :]).astype(BF16)
            top = top + _dot(a, w1_ref[HEAD_DIM * l:HEAD_DIM * (l + 1), :])
            c = (xl + p_ref[CMP_STRIDE + l:CMP_STRIDE + l + 1, :]).astype(BF16)
            bot = bot + _dot(c, w1_ref[half + HEAD_DIM * l:half + HEAD_DIM * (l + 1), :])
        hid = top + pltpu.roll(bot, n_chunk - 1, axis=0)
        return jax.nn.gelu(hid).astype(BF16)

    out = _dot(hidden(kr_ref, pk_ref, k1_ref), k2_ref[...])
    row = lax.broadcasted_iota(jnp.int32, out.shape, 0)
    ko_ref[0, 0] = jnp.where(row < n_chunk - 1, out, 0.0).astype(ko_ref.dtype)
    out_t = _dot_nt(v2t_ref[...], hidden(vr_ref, pv_ref, v1_ref))
    col = lax.broadcasted_iota(jnp.int32, out_t.shape, 1)
    vot_ref[0, 0] = jnp.where(col < n_chunk - 1, out_t, 0.0).astype(vot_ref.dtype)


def _compress(kc_raw, vc_raw, pos_k, pos_v, k1, k2, v1, v2t):
    bsz, hkv, t, dh = kc_raw.shape
    n_chunk = t // CMP_STRIDE
    raw_spec = pl.BlockSpec((1, 1, t, dh), lambda bb, hh: (bb, hh, 0, 0))
    full = lambda a: pl.BlockSpec(a.shape, lambda bb, hh: (0,) * a.ndim)
    return pl.pallas_call(
        _compress_kernel,
        grid=(bsz, hkv),
        in_specs=[raw_spec, raw_spec, full(pos_k), full(pos_v), full(k1), full(k2), full(v1), full(v2t)],
        out_specs=(pl.BlockSpec((1, 1, n_chunk, dh), lambda bb, hh: (bb, hh, 0, 0)),
                   pl.BlockSpec((1, 1, dh, n_chunk), lambda bb, hh: (bb, hh, 0, 0))),
        out_shape=(jax.ShapeDtypeStruct((bsz, hkv, n_chunk, dh), BF16),
                   jax.ShapeDtypeStruct((bsz, hkv, dh, n_chunk), BF16)),
        compiler_params=pltpu.CompilerParams(
            dimension_semantics=("parallel", "parallel"), vmem_limit_bytes=VMEM_LIMIT),
        name="compress",
    )(kc_raw, vc_raw, pos_k, pos_v, k1, k2, v1, v2t)


def _expand_kernel(tbl_ref, idx_ref, o_ref, *, gsz):
    grp = pl.program_id(0)
    idx = idx_ref[0]
    w = idx.shape[1]
    accs = [jnp.full(idx.shape, tbl_ref[MASKED, grp * gsz + g], F32) for g in range(gsz)]
    for b in range(NUM_BUCKETS):
        hit = idx == b
        for g in range(gsz):
            accs[g] = jnp.where(hit, tbl_ref[b, grp * gsz + g], accs[g])
    for g in range(gsz):
        o_ref[0, 0, :, w * g:w * (g + 1)] = accs[g]


def _expand_tables(tbl, idx, gsz):
    r0, r1, w = idx.shape
    heads = tbl.shape[1]
    tr = min(r1, 128)
    tbl_ext = jnp.concatenate([tbl.astype(F32), jnp.full((1, heads), NEG_INF, F32)], axis=0)
    return pl.pallas_call(
        functools.partial(_expand_kernel, gsz=gsz),
        grid=(heads // gsz, r0, r1 // tr),
        in_specs=[pl.BlockSpec(memory_space=pltpu.SMEM),
                  pl.BlockSpec((1, tr, w), lambda gg, a, r: (a, r, 0))],
        out_specs=pl.BlockSpec((1, 1, tr, gsz * w), lambda gg, a, r: (gg, a, r, 0)),
        out_shape=jax.ShapeDtypeStruct((heads // gsz, r0, r1, gsz * w), F32),
        compiler_params=pltpu.CompilerParams(
            dimension_semantics=("parallel", "parallel", "parallel"), vmem_limit_bytes=VMEM_LIMIT),
        name="expand_bias",
    )(tbl_ext, idx)


def _t5_bucket(dist):
    dist = jnp.maximum(dist, 0)
    max_exact = NUM_BUCKETS // 2
    d = jnp.maximum(dist, 1).astype(F32)
    ratio = jnp.log(d / max_exact) / math.log(REL_MAX_DISTANCE / max_exact)
    large = max_exact + (ratio * (NUM_BUCKETS - max_exact)).astype(jnp.int32)
    large = jnp.minimum(large, NUM_BUCKETS - 1)
    return jnp.where(dist < max_exact, dist, large)


def _bias_indices(n_cmp):
    kj = jnp.arange(KT)[:, None]
    qq = jnp.arange(NQB)[None, :]
    d = jnp.arange(-Q_PER_K, N_NEAR + 1)[:, None, None]
    dist = KT * d + qq - kj
    sel_idx = jnp.where(dist >= 0, _t5_bucket(dist), MASKED)
    d = jnp.arange(-(Q_PER_K - 1), N_WIN_TILES + 1)[:, None, None]
    dist = KT * d + qq - kj
    win_idx = jnp.where((dist >= 0) & (dist < NSA_WINDOW), _t5_bucket(dist), MASKED)
    cprime = jnp.arange(2 * n_cmp)[:, None] - (n_cmp - C_PER_Q)
    cmp_idx = _t5_bucket(qq - CMP_STRIDE * cprime - (CMP_LEN - 1))[None]
    kr = jnp.arange(2 * KT)[:, None]
    db = jnp.arange(QB)[None, :] + KT - kr
    ok = (db >= 0) & (db < SWA_WINDOW)
    ok = jnp.stack([ok, ok & (kr >= KT)])
    swa_idx = jnp.where(ok, _t5_bucket(db)[None], MASKED)
    i32 = lambda a: a.astype(jnp.int32)
    return i32(sel_idx), i32(win_idx), i32(cmp_idx), i32(swa_idx)


def _softmax_pv(s_tiles, vt_tiles, ones):
    m = s_tiles[0].max(axis=0, keepdims=True)
    for s in s_tiles[1:]:
        m = jnp.maximum(m, s.max(axis=0, keepdims=True))
    acc = None
    for s, vt in zip(s_tiles, vt_tiles):
        p = jnp.exp(s - m).astype(BF16)
        vx = jnp.concatenate([vt, ones[:, :vt.shape[1]]], axis=0)
        part = _dot(vx, p)
        acc = part if acc is None else acc + part
    return acc[:HEAD_DIM] * (1.0 / acc[HEAD_DIM:HEAD_DIM + 1])


def _nsa_kernel(qt_ref, gt_ref, kc_ref, vct_ref, ks_ref, kw_ref, vst_ref, vwt_ref, et_ref,
                bc_ref, sb_ref, wb_ref, ov_ref, o_ref,
                kext_scr, rhs_scr, s_scr, mt_scr, p_scr, al_scr, m_scr, acc_scr):
    qi = pl.program_id(2)
    t0 = qi * NQB
    n_cmp = kc_ref.shape[2]
    n_blk = ks_ref.shape[2] // SEL_BLOCK
    grp = NSA_GROUP
    lanes = grp * NQB

    @pl.when(qi == 0)
    def _():
        kext_scr[:, 0:128] = et_ref[...]
        kext_scr[:, 128:128 + HEAD_DIM] = ks_ref[0, 0]
        kext_scr[:, 128 + HEAD_DIM:] = jnp.zeros((kext_scr.shape[0], 128 - HEAD_DIM), BF16)

    qt = qt_ref[0]
    qst = jnp.concatenate([qt[HEAD_DIM * g:HEAD_DIM * (g + 1)] for g in range(grp)], axis=1)
    ones = jnp.ones((16, 2 * KT), BF16)

    start = pl.multiple_of(n_cmp - C_PER_Q - C_PER_Q * qi, 8)
    cr = lax.broadcasted_iota(jnp.int32, (n_cmp, NQB), 0)
    cq = lax.broadcasted_iota(jnp.int32, (n_cmp, NQB), 1)
    hidden = jnp.where((t0 + cq) >= cr * CMP_STRIDE + CMP_LEN - 1, 0.0, NEG_INF)
    s = _dot(kc_ref[0, 0], qst) + bc_ref[0, 0, pl.ds(start, n_cmp), :]
    s = s + jnp.concatenate([hidden] * grp, axis=1)
    e = jnp.exp(s - s.max(axis=0, keepdims=True))
    tq = t0 + (lax.broadcasted_iota(jnp.int32, (1, lanes), 1) & (NQB - 1))
    inv = jnp.where(tq >= CMP_LEN - 1, 1.0 / e.sum(axis=0, keepdims=True), 0.0)
    p = e * inv
    o_c = _dot(vct_ref[0, 0], p.astype(BF16))

    s_tiles, vt_tiles = [], []
    n_win_entries = Q_PER_K - 1 + N_WIN_TILES
    for d in range(-(Q_PER_K - 1), N_WIN_TILES):
        jj = Q_PER_K * qi - d
        k0 = pl.multiple_of(jnp.maximum(jj, 0) * KT, KT)
        entry = jnp.where(jj >= 0, d + Q_PER_K - 1, n_win_entries)
        s_tiles.append(_dot(kw_ref[0, 0, pl.ds(k0, KT), :], qst) + wb_ref[0, entry])
        vt_tiles.append(vwt_ref[0, :, pl.ds(k0, KT)])
    o_w = _softmax_pv(s_tiles, vt_tiles, ones)

    p_sum = p[:, 0:NQB]
    for g in range(1, grp):
        p_sum = p_sum + p[:, NQB * g:NQB * (g + 1)]
    p_hi = p_sum.astype(BF16)
    p_lo = (p_sum - p_hi.astype(F32)).astype(BF16)
    imp = (_dot(ov_ref[...], p_hi) + _dot(ov_ref[...], p_lo))[:n_blk]
    jb = lax.broadcasted_iota(jnp.int32, (n_blk, NQB), 0)
    blk_t = (t0 + lax.broadcasted_iota(jnp.int32, (n_blk, NQB), 1)) >> SEL_SHIFT
    forced = (jb == 0) | (jb == blk_t) | (jb == blk_t - 1)
    val = jnp.where(forced, imp + FORCE_SCORE, imp)
    val = jnp.where(jb > blk_t, NEG_INF, val)
    n_grp = n_blk // 8
    vals = [val[8 * k:8 * (k + 1)] for k in range(n_grp)]
    ranks = [jnp.zeros((8, NQB), jnp.int32) for _ in range(n_grp)]
    row8 = lax.broadcasted_iota(jnp.int32, (8, NQB), 0)
    for j in range(n_blk):
        r = jnp.broadcast_to(val[j:j + 1], (8, NQB))
        for k in range(n_grp):
            if 8 * k > j:
                beats = r >= vals[k]
            elif 8 * k + 7 <= j:
                beats = r > vals[k]
            else:
                beats = (r > vals[k]) | ((r == vals[k]) & (row8 + 8 * k > j))
            ranks[k] = ranks[k] + jnp.where(beats, 1, 0)
    n_sel = min(N_SEL, n_blk)
    drop = [jnp.where(rk < n_sel, 0.0, NEG_INF) for rk in ranks]
    drop = jnp.concatenate(drop + [jnp.zeros((128 - n_blk, NQB), F32)], axis=0).astype(BF16)
    rhs = jnp.concatenate([jnp.concatenate([drop] * grp, axis=1), qst,
                           jnp.zeros((128 - HEAD_DIM, lanes), BF16)], axis=0)

    rhs_scr[...] = rhs

    tk = 2 * KT
    last_tile = kext_scr.shape[0] // tk - 1

    def logits(i):
        k0 = pl.multiple_of(jnp.minimum(i, last_tile) * tk, tk)
        s = _dot(kext_scr[pl.ds(k0, tk), :], rhs_scr[...])
        d = Q_PER_K * qi - 2 * i
        e1 = jnp.clip(d, -Q_PER_K, N_NEAR) + Q_PER_K
        e2 = jnp.clip(d - 1, -Q_PER_K, N_NEAR) + Q_PER_K
        s = s + jnp.concatenate([sb_ref[0, e1], sb_ref[0, e2]], axis=0)
        return s, s.max(axis=0, keepdims=True)

    def softmax(s, mt, m_prev):
        m_new = jnp.maximum(m_prev, mt)
        return jnp.exp(s - m_new).astype(BF16), jnp.exp(m_prev - m_new), m_new

    def pv(i, p, alpha, acc):
        k0 = pl.multiple_of(jnp.clip(i, 0, last_tile) * tk, tk)
        vx = jnp.concatenate([vst_ref[0, :, pl.ds(k0, tk)], ones], axis=0)
        return acc * alpha + _dot(vx, p)

    s0, mt0 = logits(0)
    s1, mt1 = logits(1)
    p0, a0, m0 = softmax(s0, mt0, jnp.full((1, lanes), NEG_INF, F32))
    s_scr[1], mt_scr[1] = s1, mt1
    p_scr[0], al_scr[0] = p0, a0
    m_scr[...] = m0
    acc_scr[...] = jnp.zeros(acc_scr.shape, F32)

    def sel_body(i, carry):
        par = i & 1
        p_old, a_old, acc = p_scr[par], al_scr[par], acc_scr[...]
        s_cur, mt_cur, m_prev = s_scr[1 - par], mt_scr[1 - par], m_scr[...]
        s_new, mt_new = logits(i)
        acc_scr[...] = pv(i - 2, p_old, a_old, acc)
        p_cur, a_cur, m_new = softmax(s_cur, mt_cur, m_prev)
        p_scr[1 - par], al_scr[1 - par] = p_cur, a_cur
        m_scr[...] = m_new
        s_scr[par], mt_scr[par] = s_new, mt_new
        return carry

    n_tiles = jnp.maximum((Q_PER_K * (qi + 1) + 1) // 2, 2)
    lax.fori_loop(2, n_tiles, sel_body, 0)
    par = n_tiles & 1
    acc = pv(n_tiles - 2, p_scr[par], al_scr[par], acc_scr[...])
    p_last, a_last, _ = softmax(s_scr[1 - par], mt_scr[1 - par], m_scr[...])
    acc = pv(n_tiles - 1, p_last, a_last, acc)
    o_s = acc[:HEAD_DIM] * (1.0 / acc[HEAD_DIM:HEAD_DIM + 1])

    gate = jax.nn.sigmoid(gt_ref[0])
    outs = []
    for g in range(grp):
        sl = slice(NQB * g, NQB * (g + 1))
        outs.append(gate[3 * g:3 * g + 1] * o_c[:, sl] + gate[3 * g + 1:3 * g + 2] * o_s[:, sl]
                    + gate[3 * g + 2:3 * g + 3] * o_w[:, sl])
    o_ref[0] = jnp.concatenate(outs, axis=0).T.astype(o_ref.dtype)


def _nsa_attention(qat, gtt, kc, vct, ks, kw, vst, vwt, et, bct, sbt, wbt, ovt):
    bsz, _, t = qat.shape
    hkv = NSA_KV_HEADS
    n_cmp = kc.shape[2]
    gw = NSA_GROUP * HEAD_DIM
    lanes = NSA_GROUP * NQB
    kv_spec = pl.BlockSpec((1, 1, t, HEAD_DIM), lambda bb, hh, i: (bb, hh, 0, 0))
    vt_spec = pl.BlockSpec((1, HEAD_DIM, t), lambda bb, hh, i: (bb, hh, 0))
    tbl = lambda a: pl.BlockSpec((1,) + a.shape[1:], lambda bb, hh, i: (hh, 0, 0, 0))
    const = lambda a: pl.BlockSpec(a.shape, lambda bb, hh, i: (0, 0))
    return pl.pallas_call(
        _nsa_kernel,
        grid=(bsz, hkv, t // NQB),
        in_specs=[
            pl.BlockSpec((1, gw, NQB), lambda bb, hh, i: (bb, hh, i)),
            pl.BlockSpec((1, GATE_ROWS, NQB), lambda bb, hh, i: (bb, hh, i)),
            pl.BlockSpec((1, 1, n_cmp, HEAD_DIM), lambda bb, hh, i: (bb, hh, 0, 0)),
            pl.BlockSpec((1, 1, HEAD_DIM, n_cmp), lambda bb, hh, i: (bb, hh, 0, 0)),
            kv_spec, kv_spec, vt_spec, vt_spec, const(et),
            tbl(bct), tbl(sbt), tbl(wbt), const(ovt),
        ],
        out_specs=pl.BlockSpec((1, NQB, gw), lambda bb, hh, i: (bb, i, hh)),
        out_shape=jax.ShapeDtypeStruct((bsz, t, NSA_HEADS * HEAD_DIM), BF16),
        scratch_shapes=[pltpu.VMEM((t, 256), BF16),
                        pltpu.VMEM((256, lanes), BF16),
                        pltpu.VMEM((2, 2 * KT, lanes), F32),
                        pltpu.VMEM((2, 1, lanes), F32),
                        pltpu.VMEM((2, 2 * KT, lanes), BF16),
                        pltpu.VMEM((2, 1, lanes), F32),
                        pltpu.VMEM((1, lanes), F32),
                        pltpu.VMEM((HEAD_DIM + 16, lanes), F32)],
        compiler_params=pltpu.CompilerParams(
            dimension_semantics=("parallel", "parallel", "arbitrary"), vmem_limit_bytes=VMEM_LIMIT),
        name="nsa_attention",
    )(qat, gtt, kc, vct, ks, kw, vst, vwt, et, bct, sbt, wbt, ovt)


def _swa_kernel(sink_ref, qt_ref, kp_ref, kn_ref, vp_ref, vn_ref, wb_ref, o_ref):
    qi = pl.program_id(2)
    qt = qt_ref[0]
    qst = jnp.concatenate([qt[HEAD_DIM * g:HEAD_DIM * (g + 1)] for g in range(SWA_GROUP)], axis=1)
    k = jnp.concatenate([kp_ref[0, 0], kn_ref[0, 0]], axis=0)
    vt = jnp.concatenate([vp_ref[0], vn_ref[0]], axis=1)
    s = _dot(k, qst) + wb_ref[0, jnp.where(qi == 0, 1, 0)]
    sink = sink_ref[0]
    m = jnp.maximum(s.max(axis=0, keepdims=True), sink)
    p = jnp.exp(s - m).astype(BF16)
    vx = jnp.concatenate([vt, jnp.ones((16, 2 * KT), BF16)], axis=0)
    acc = _dot(vx, p)
    o = acc[:HEAD_DIM] * (1.0 / (acc[HEAD_DIM:HEAD_DIM + 1] + jnp.exp(sink - m)))
    o = jnp.concatenate([o[:, QB * g:QB * (g + 1)] for g in range(SWA_GROUP)], axis=0)
    o_ref[0] = o.T.astype(o_ref.dtype)


def _swa_attention(sink_rows, qbt, kb, vbt, wbb):
    bsz, _, t = qbt.shape
    gw = SWA_GROUP * HEAD_DIM
    lanes = SWA_GROUP * QB
    kprev = pl.BlockSpec((1, 1, KT, HEAD_DIM), lambda bb, hh, i: (bb, hh, jnp.maximum(i - 1, 0), 0))
    kcur = pl.BlockSpec((1, 1, KT, HEAD_DIM), lambda bb, hh, i: (bb, hh, i, 0))
    vprev = pl.BlockSpec((1, HEAD_DIM, KT), lambda bb, hh, i: (bb, hh, jnp.maximum(i - 1, 0)))
    vcur = pl.BlockSpec((1, HEAD_DIM, KT), lambda bb, hh, i: (bb, hh, i))
    return pl.pallas_call(
        _swa_kernel,
        grid=(bsz, SWA_KV_HEADS, t // QB),
        in_specs=[
            pl.BlockSpec((1, 1, lanes), lambda bb, hh, i: (hh, 0, 0)),
            pl.BlockSpec((1, gw, QB), lambda bb, hh, i: (bb, hh, i)),
            kprev, kcur, vprev, vcur,
            pl.BlockSpec((1,) + wbb.shape[1:], lambda bb, hh, i: (hh, 0, 0, 0)),
        ],
        out_specs=pl.BlockSpec((1, QB, gw), lambda bb, hh, i: (bb, i, hh)),
        out_shape=jax.ShapeDtypeStruct((bsz, t, SWA_HEADS * HEAD_DIM), BF16),
        compiler_params=pltpu.CompilerParams(
            dimension_semantics=("parallel", "parallel", "arbitrary"), vmem_limit_bytes=VMEM_LIMIT),
        name="swa_attention",
    )(sink_rows, qbt, kb, kb, vbt, vbt, wbb)


def _outproj_kernel(x_ref, oa_ref, ob_ref, w_ref, o_ref):
    na = oa_ref.shape[1]
    o_ref[...] = x_ref[...] + _dot(oa_ref[...], w_ref[:na, :]) + _dot(ob_ref[...], w_ref[na:, :])


def _outproj(x, oa, ob, w):
    n, d = x.shape
    tm = min(ROW_TILE, n)
    row = lambda width: pl.BlockSpec((tm, width), lambda i: (i, 0))
    return pl.pallas_call(
        _outproj_kernel,
        grid=(n // tm,),
        in_specs=[row(d), row(oa.shape[1]), row(ob.shape[1]), pl.BlockSpec(w.shape, lambda i: (0, 0))],
        out_specs=row(d),
        out_shape=jax.ShapeDtypeStruct((n, d), F32),
        compiler_params=pltpu.CompilerParams(
            dimension_semantics=("parallel",), vmem_limit_bytes=VMEM_LIMIT),
        name="outproj",
    )(x, oa, ob, w)


def _ffn_kernel(x_ref, g_ref, wg_ref, wu_ref, wd_ref, gf_ref, o_ref, h_scr, acc_scr, *, final_norm):
    f = pl.program_id(1)

    @pl.when(f == 0)
    def _():
        h_scr[...] = _rms(x_ref[...], g_ref[...]).astype(BF16)
        acc_scr[...] = jnp.zeros(acc_scr.shape, F32)

    h = h_scr[...]
    gate = _dot(h, wg_ref[...])
    act = (gate * jax.nn.sigmoid(gate) * _dot(h, wu_ref[...])).astype(BF16)
    acc_scr[...] += _dot(act, wd_ref[...])

    @pl.when(f == pl.num_programs(1) - 1)
    def _():
        y = x_ref[...] + acc_scr[...]
        o_ref[...] = _rms(y, gf_ref[...]) if final_norm else y


def _ffn(x, g, wg, wu, wd, g_final, final_norm):
    n, d = x.shape
    dff = wg.shape[1]
    tm = min(ROW_TILE, n)
    return pl.pallas_call(
        functools.partial(_ffn_kernel, final_norm=final_norm),
        grid=(n // tm, dff // FF_TILE),
        in_specs=[
            pl.BlockSpec((tm, d), lambda i, f: (i, 0)),
            pl.BlockSpec((1, d), lambda i, f: (0, 0)),
            pl.BlockSpec((d, FF_TILE), lambda i, f: (0, f)),
            pl.BlockSpec((d, FF_TILE), lambda i, f: (0, f)),
            pl.BlockSpec((FF_TILE, d), lambda i, f: (f, 0)),
            pl.BlockSpec((1, d), lambda i, f: (0, 0)),
        ],
        out_specs=pl.BlockSpec((tm, d), lambda i, f: (i, 0)),
        out_shape=jax.ShapeDtypeStruct((n, d), F32),
        scratch_shapes=[pltpu.VMEM((tm, d), BF16), pltpu.VMEM((tm, d), F32)],
        compiler_params=pltpu.CompilerParams(
            dimension_semantics=("parallel", "arbitrary"), vmem_limit_bytes=VMEM_LIMIT),
        name="ffn",
    )(x, g, wg, wu, wd, g_final)


def _gate_rows(a):
    depth, _, n = a.shape
    a = a.reshape(depth, NSA_KV_HEADS, 3 * NSA_GROUP, n)
    a = jnp.pad(a, ((0, 0), (0, 0), (0, GATE_ROWS - 3 * NSA_GROUP), (0, 0)))
    return a.reshape(depth, N_GATES, n)


def _overlap_t(n_cmp, n_blk):
    ci = jnp.arange(n_cmp)[None, :] * CMP_STRIDE
    sj = jnp.arange(128)[:, None] * SEL_BLOCK
    ov = (ci < sj + SEL_BLOCK) & (ci + CMP_LEN > sj) & (jnp.arange(128)[:, None] < n_blk)
    return (ov & (jnp.arange(n_cmp)[None, :] < n_cmp - 1)).astype(BF16)


def kernel(x, rel_bias, norm_mix, norm_ffn, w_in, b_in, cmp_pos_k, cmp_pos_v, cmp_k_w1, cmp_k_w2,
           cmp_v_w1, cmp_v_w2, sinks, w_out, w_gate, w_up, w_down, norm_final):
    bsz, t, d = x.shape
    depth = w_in.shape[0]
    n_cmp = t // CMP_STRIDE
    sel_idx, win_idx, cmp_idx, swa_idx = _bias_indices(n_cmp)
    tbl_a, tbl_b = rel_bias[:, :NSA_HEADS], rel_bias[:, NSA_HEADS:]
    sbt = _expand_tables(tbl_a, sel_idx, NSA_GROUP)
    wbt = _expand_tables(tbl_a, win_idx, NSA_GROUP)
    bct = _expand_tables(tbl_a, cmp_idx, NSA_GROUP)
    wbb = _expand_tables(tbl_b, swa_idx, SWA_GROUP)
    ovt = _overlap_t(n_cmp, t // SEL_BLOCK)
    et = (jnp.arange(t)[:, None] // SEL_BLOCK == jnp.arange(128)[None, :]).astype(BF16)
    w_in_t = jnp.swapaxes(w_in.astype(BF16), -1, -2)
    b_row = jnp.concatenate([b_in[:, a:b] for a, b in ROWMAJOR], axis=-1).astype(F32)
    w_gt, b_gt = _gate_rows(w_in_t[:, COL_GT[0]:COL_GT[1], :]), _gate_rows(b_in[:, COL_GT[0]:COL_GT[1], None])
    for layer in range(depth):
        qat, kc_raw, vc_raw, ks, kw, vst, vwt, gtt, qbt, kb, vbt = _inproj(
            x, norm_mix[layer][None, :], w_in_t[layer], b_row[layer][None, :],
            b_in[layer].astype(F32)[:, None], w_gt[layer], b_gt[layer].astype(F32))
        kc, vct = _compress(kc_raw, vc_raw, cmp_pos_k[layer], cmp_pos_v[layer],
                            cmp_k_w1[layer].astype(BF16), cmp_k_w2[layer].astype(BF16),
                            cmp_v_w1[layer].astype(BF16), cmp_v_w2[layer].T.astype(BF16))
        o_a = _nsa_attention(qat, gtt, kc, vct, ks, kw, vst, vwt, et, bct, sbt, wbt, ovt)
        sink_rows = jnp.repeat(sinks[layer].astype(F32), QB).reshape(SWA_KV_HEADS, 1, SWA_GROUP * QB)
        o_b = _swa_attention(sink_rows, qbt, kb, vbt, wbb)
        x2 = _outproj(x.reshape(bsz * t, d), o_a.reshape(bsz * t, -1), o_b.reshape(bsz * t, -1),
                      w_out[layer].astype(BF16))
        x2 = _ffn(x2, norm_ffn[layer][None, :], w_gate[layer].astype(BF16), w_up[layer].astype(BF16),
                  w_down[layer].astype(BF16), norm_final[None, :], layer == depth - 1)
        x = x2.reshape(bsz, t, d)
    return x
```

```python
import functools
import math

import jax
import jax.numpy as jnp
from jax import lax
from jax.experimental import pallas as pl
from jax.experimental.pallas import tpu as pltpu

F32 = jnp.float32
BF16 = jnp.bfloat16

D_MODEL = 2048
HEAD_DIM = 64
NSA_HEADS = 16
NSA_KV_HEADS = 4
NSA_GROUP = 4
SWA_HEADS = 16
SWA_KV_HEADS = 2
SWA_GROUP = 8
CMP_LEN = 32
CMP_STRIDE = 16
CMP_HIDDEN = 256
SEL_BLOCK = 64
SEL_SHIFT = 6
N_SEL = 16
NSA_WINDOW = 512
SWA_WINDOW = 128
NUM_BUCKETS = 32
REL_MAX_DISTANCE = 1024
RMS_EPS = 1e-5
NEG_INF = -1e30
FORCE_SCORE = 1e6

QB = 128
NQB = 256
KT = 128
SEL_TILE = 4 * KT
Q_PER_K = NQB // KT
C_PER_Q = NQB // CMP_STRIDE
N_NEAR = 8
N_WIN_TILES = NSA_WINDOW // KT + 1
MASKED = NUM_BUCKETS
ROW_TILE = 512
FF_TILE = 512
VMEM_LIMIT = 56 * 1024 * 1024

COL_QA = (0, 1024)
COL_KC, COL_VC = (1024, 1280), (1280, 1536)
COL_KS, COL_VS = (1536, 1792), (1792, 2048)
COL_KW, COL_VW = (2048, 2304), (2304, 2560)
COL_GT = (2560, 2608)
COL_QB = (2608, 3632)
COL_KB, COL_VB = (3632, 3760), (3760, 3888)
GATE_ROWS = 16
N_GATES = NSA_KV_HEADS * GATE_ROWS
ROWMAJOR = (COL_KC, COL_VC, COL_KS, COL_KW, COL_KB)


def _dot(a, b):
    return jnp.dot(a, b, preferred_element_type=F32)


def _dot_nt(a, b):
    return lax.dot_general(a, b, (((1,), (1,)), ((), ())), preferred_element_type=F32)


def _rms(x, g):
    return x * lax.rsqrt(jnp.mean(x * x, axis=-1, keepdims=True) + RMS_EPS) * g


def _inproj_kernel(x_ref, g_ref, wt_ref, br_ref, bc_ref, wg_ref, bg_ref,
                   qat_ref, kc_ref, vc_ref, ks_ref, kw_ref, vst_ref, vwt_ref, gtt_ref,
                   qbt_ref, kb_ref, vbt_ref):
    h = _rms(x_ref[0], g_ref[...]).astype(BF16)

    def colmajor(cols):
        a, b = cols
        return _dot_nt(wt_ref[a:b, :], h) + bc_ref[a:b, :]

    off = 0
    for (a, b), ref in zip(ROWMAJOR, (kc_ref, vc_ref, ks_ref, kw_ref, kb_ref)):
        r = _dot_nt(h, wt_ref[a:b, :]) + br_ref[:, off:off + b - a]
        off += b - a
        for hh in range((b - a) // HEAD_DIM):
            ref[0, hh] = r[:, HEAD_DIM * hh:HEAD_DIM * (hh + 1)].astype(ref.dtype)
    qat_ref[0] = (colmajor(COL_QA) * (HEAD_DIM ** -0.5)).astype(qat_ref.dtype)
    qbt_ref[0] = (colmajor(COL_QB) * (HEAD_DIM ** -0.5)).astype(qbt_ref.dtype)
    vst_ref[0] = colmajor(COL_VS).astype(vst_ref.dtype)
    vwt_ref[0] = colmajor(COL_VW).astype(vwt_ref.dtype)
    vbt_ref[0] = colmajor(COL_VB).astype(vbt_ref.dtype)
    gtt_ref[0] = _dot_nt(wg_ref[...], h) + bg_ref[...]


def _inproj(x, g, wt, b_row, b_col, wg, bg, layer):
    bsz, t, d = x.shape
    tm = min(ROW_TILE, t)
    kv4 = lambda dt: jax.ShapeDtypeStruct((bsz, NSA_KV_HEADS, t, HEAD_DIM), dt)
    kv2 = jax.ShapeDtypeStruct((bsz, SWA_KV_HEADS, t, HEAD_DIM), BF16)
    colm = lambda rows, dt: jax.ShapeDtypeStruct((bsz, rows, t), dt)
    out_shape = (colm(1024, BF16), kv4(F32), kv4(F32), kv4(BF16), kv4(BF16),
                 colm(256, BF16), colm(256, BF16), colm(N_GATES, F32),
                 colm(1024, BF16), kv2, colm(128, BF16))
    kv4_spec = pl.BlockSpec((1, NSA_KV_HEADS, tm, HEAD_DIM), lambda bb, i: (bb, 0, i, 0))
    kv2_spec = pl.BlockSpec((1, SWA_KV_HEADS, tm, HEAD_DIM), lambda bb, i: (bb, 0, i, 0))
    colm_spec = lambda rows: pl.BlockSpec((1, rows, tm), lambda bb, i: (bb, 0, i))
    const = lambda a: pl.BlockSpec(a.shape, lambda bb, i: (0, 0), pipeline_mode=pl.Buffered(1))
    return pl.pallas_call(
        _inproj_kernel,
        grid=(bsz, t // tm),
        in_specs=[pl.BlockSpec((1, tm, d), lambda bb, i: (bb, i, 0)),
                  const(g),
                  pl.BlockSpec((None,) + wt.shape[1:], lambda bb, i: (layer, 0, 0), pipeline_mode=pl.Buffered(1)),
                  const(b_row), const(b_col), const(wg), const(bg)],
        out_specs=(colm_spec(1024), kv4_spec, kv4_spec, kv4_spec, kv4_spec,
                   colm_spec(256), colm_spec(256), colm_spec(N_GATES),
                   colm_spec(1024), kv2_spec, colm_spec(128)),
        out_shape=out_shape,
        compiler_params=pltpu.CompilerParams(
            dimension_semantics=("parallel", "parallel"), vmem_limit_bytes=VMEM_LIMIT),
        name="inproj",
    )(x, g, wt, b_row, b_col, wg, bg)


def _compress_kernel(kr_ref, vr_ref, pk_ref, pv_ref, k1_ref, k2_ref, v1_ref, v2t_ref,
                     ko_ref, vot_ref):
    n_chunk = kr_ref.shape[2] // CMP_STRIDE
    half = CMP_STRIDE * HEAD_DIM

    def hidden(r_ref, p_ref, w1_ref):
        top = jnp.zeros((n_chunk, CMP_HIDDEN), F32)
        bot = jnp.zeros((n_chunk, CMP_HIDDEN), F32)
        for l in range(CMP_STRIDE):
            xl = r_ref[0, 0, pl.ds(l, n_chunk, stride=CMP_STRIDE), :]
            a = (xl + p_ref[l:l + 1, :]).astype(BF16)
            top = top + _dot(a, w1_ref[HEAD_DIM * l:HEAD_DIM * (l + 1), :])
            c = (xl + p_ref[CMP_STRIDE + l:CMP_STRIDE + l + 1, :]).astype(BF16)
            bot = bot + _dot(c, w1_ref[half + HEAD_DIM * l:half + HEAD_DIM * (l + 1), :])
        hid = top + pltpu.roll(bot, n_chunk - 1, axis=0)
        return jax.nn.gelu(hid).astype(BF16)

    out = _dot(hidden(kr_ref, pk_ref, k1_ref), k2_ref[...])
    row = lax.broadcasted_iota(jnp.int32, out.shape, 0)
    ko_ref[0, 0] = jnp.where(row < n_chunk - 1, out, 0.0).astype(ko_ref.dtype)
    out_t = _dot_nt(v2t_ref[...], hidden(vr_ref, pv_ref, v1_ref))
    col = lax.broadcasted_iota(jnp.int32, out_t.shape, 1)
    vot_ref[0, 0] = jnp.where(col < n_chunk - 1, out_t, 0.0).astype(vot_ref.dtype)


def _compress(kc_raw, vc_raw, pos_k, pos_v, k1, k2, v1, v2t):
    bsz, hkv, t, dh = kc_raw.shape
    n_chunk = t // CMP_STRIDE
    raw_spec = pl.BlockSpec((1, 1, t, dh), lambda bb, hh: (bb, hh, 0, 0))
    full = lambda a: pl.BlockSpec(a.shape, lambda bb, hh: (0,) * a.ndim)
    return pl.pallas_call(
        _compress_kernel,
        grid=(bsz, hkv),
        in_specs=[raw_spec, raw_spec, full(pos_k), full(pos_v), full(k1), full(k2), full(v1), full(v2t)],
        out_specs=(pl.BlockSpec((1, 1, n_chunk, dh), lambda bb, hh: (bb, hh, 0, 0)),
                   pl.BlockSpec((1, 1, dh, n_chunk), lambda bb, hh: (bb, hh, 0, 0))),
        out_shape=(jax.ShapeDtypeStruct((bsz, hkv, n_chunk, dh), BF16),
                   jax.ShapeDtypeStruct((bsz, hkv, dh, n_chunk), BF16)),
        compiler_params=pltpu.CompilerParams(
            dimension_semantics=("parallel", "parallel"), vmem_limit_bytes=VMEM_LIMIT),
        name="compress",
    )(kc_raw, vc_raw, pos_k, pos_v, k1, k2, v1, v2t)


def _expand_kernel(tbl_ref, idx_ref, o_ref, *, gsz):
    grp = pl.program_id(0)
    idx = idx_ref[0]
    w = idx.shape[1]
    accs = [jnp.full(idx.shape, tbl_ref[MASKED, grp * gsz + g], F32) for g in range(gsz)]
    for b in range(NUM_BUCKETS):
        hit = idx == b
        for g in range(gsz):
            accs[g] = jnp.where(hit, tbl_ref[b, grp * gsz + g], accs[g])
    for g in range(gsz):
        o_ref[0, 0, :, w * g:w * (g + 1)] = accs[g]


def _expand_tables(tbl, idx, gsz):
    r0, r1, w = idx.shape
    heads = tbl.shape[1]
    tr = min(r1, 128)
    tbl_ext = jnp.concatenate([tbl.astype(F32), jnp.full((1, heads), NEG_INF, F32)], axis=0)
    return pl.pallas_call(
        functools.partial(_expand_kernel, gsz=gsz),
        grid=(heads // gsz, r0, r1 // tr),
        in_specs=[pl.BlockSpec(memory_space=pltpu.SMEM),
                  pl.BlockSpec((1, tr, w), lambda gg, a, r: (a, r, 0))],
        out_specs=pl.BlockSpec((1, 1, tr, gsz * w), lambda gg, a, r: (gg, a, r, 0)),
        out_shape=jax.ShapeDtypeStruct((heads // gsz, r0, r1, gsz * w), F32),
        compiler_params=pltpu.CompilerParams(
            dimension_semantics=("parallel", "parallel", "parallel"), vmem_limit_bytes=VMEM_LIMIT),
        name="expand_bias",
    )(tbl_ext, idx)


def _t5_bucket(dist):
    dist = jnp.maximum(dist, 0)
    max_exact = NUM_BUCKETS // 2
    d = jnp.maximum(dist, 1).astype(F32)
    ratio = jnp.log(d / max_exact) / math.log(REL_MAX_DISTANCE / max_exact)
    large = max_exact + (ratio * (NUM_BUCKETS - max_exact)).astype(jnp.int32)
    large = jnp.minimum(large, NUM_BUCKETS - 1)
    return jnp.where(dist < max_exact, dist, large)


def _bias_indices(n_cmp):
    kj = jnp.arange(KT)[:, None]
    qq = jnp.arange(NQB)[None, :]
    d = jnp.arange(-Q_PER_K, N_NEAR + 1)[:, None, None]
    dist = KT * d + qq - kj
    sel_idx = jnp.where(dist >= 0, _t5_bucket(dist), MASKED)
    d = jnp.arange(-(Q_PER_K - 1), N_WIN_TILES + 1)[:, None, None]
    dist = KT * d + qq - kj
    win_idx = jnp.where((dist >= 0) & (dist < NSA_WINDOW), _t5_bucket(dist), MASKED)
    cprime = jnp.arange(2 * n_cmp)[:, None] - (n_cmp - C_PER_Q)
    cmp_idx = _t5_bucket(qq - CMP_STRIDE * cprime - (CMP_LEN - 1))[None]
    kr = jnp.arange(2 * KT)[:, None]
    db = jnp.arange(QB)[None, :] + KT - kr
    ok = (db >= 0) & (db < SWA_WINDOW)
    ok = jnp.stack([ok, ok & (kr >= KT)])
    swa_idx = jnp.where(ok, _t5_bucket(db)[None], MASKED)
    i32 = lambda a: a.astype(jnp.int32)
    return i32(sel_idx), i32(win_idx), i32(cmp_idx), i32(swa_idx)


def _softmax_pv(s_tiles, vt_tiles, ones):
    m = s_tiles[0].max(axis=0, keepdims=True)
    for s in s_tiles[1:]:
        m = jnp.maximum(m, s.max(axis=0, keepdims=True))
    acc = None
    for s, vt in zip(s_tiles, vt_tiles):
        p = jnp.exp(s - m).astype(BF16)
        vx = jnp.concatenate([vt, ones[:, :vt.shape[1]]], axis=0)
        part = _dot(vx, p)
        acc = part if acc is None else acc + part
    return acc[:HEAD_DIM] * (1.0 / acc[HEAD_DIM:HEAD_DIM + 1])


def _nsa_kernel(qt_ref, gt_ref, kc_ref, vct_ref, ks_ref, kw_ref, vst_ref, vwt_ref, et_ref,
                bc_ref, sb_ref, wb_ref, ov_ref, o_ref,
                kext_scr, rhs_scr, s_scr, mt_scr, p_scr, al_scr, m_scr, acc_scr):
    qi = pl.program_id(2)
    t0 = qi * NQB
    n_cmp = kc_ref.shape[2]
    n_blk = ks_ref.shape[2] // SEL_BLOCK
    grp = NSA_GROUP
    lanes = grp * NQB

    @pl.when(qi == 0)
    def _():
        kext_scr[:, 0:128] = et_ref[...]
        kext_scr[:, 128:128 + HEAD_DIM] = ks_ref[0, 0]
        kext_scr[:, 128 + HEAD_DIM:] = jnp.zeros((kext_scr.shape[0], 128 - HEAD_DIM), BF16)

    qt = qt_ref[0]
    qst = jnp.concatenate([qt[HEAD_DIM * g:HEAD_DIM * (g + 1)] for g in range(grp)], axis=1)
    ones = jnp.ones((16, SEL_TILE), BF16)

    start = pl.multiple_of(n_cmp - C_PER_Q - C_PER_Q * qi, 8)
    cr = lax.broadcasted_iota(jnp.int32, (n_cmp, NQB), 0)
    cq = lax.broadcasted_iota(jnp.int32, (n_cmp, NQB), 1)
    hidden = jnp.where((t0 + cq) >= cr * CMP_STRIDE + CMP_LEN - 1, 0.0, NEG_INF)
    s = _dot(kc_ref[0, 0], qst) + bc_ref[0, 0, pl.ds(start, n_cmp), :]
    s = s + jnp.concatenate([hidden] * grp, axis=1)
    e = jnp.exp(s - s.max(axis=0, keepdims=True))
    tq = t0 + (lax.broadcasted_iota(jnp.int32, (1, lanes), 1) & (NQB - 1))
    inv = jnp.where(tq >= CMP_LEN - 1, 1.0 / e.sum(axis=0, keepdims=True), 0.0)
    p = e * inv
    o_c = _dot(vct_ref[0, 0], p.astype(BF16))

    s_tiles, vt_tiles = [], []
    n_win_entries = Q_PER_K - 1 + N_WIN_TILES
    for d in range(-(Q_PER_K - 1), N_WIN_TILES):
        jj = Q_PER_K * qi - d
        k0 = pl.multiple_of(jnp.maximum(jj, 0) * KT, KT)
        entry = jnp.where(jj >= 0, d + Q_PER_K - 1, n_win_entries)
        s_tiles.append(_dot(kw_ref[0, 0, pl.ds(k0, KT), :], qst) + wb_ref[0, entry])
        vt_tiles.append(vwt_ref[0, :, pl.ds(k0, KT)])
    o_w = _softmax_pv(s_tiles, vt_tiles, ones)

    p_sum = p[:, 0:NQB]
    for g in range(1, grp):
        p_sum = p_sum + p[:, NQB * g:NQB * (g + 1)]
    p_hi = p_sum.astype(BF16)
    p_lo = (p_sum - p_hi.astype(F32)).astype(BF16)
    imp = (_dot(ov_ref[...], p_hi) + _dot(ov_ref[...], p_lo))[:n_blk]
    jb = lax.broadcasted_iota(jnp.int32, (n_blk, NQB), 0)
    blk_t = (t0 + lax.broadcasted_iota(jnp.int32, (n_blk, NQB), 1)) >> SEL_SHIFT
    forced = (jb == 0) | (jb == blk_t) | (jb == blk_t - 1)
    val = jnp.where(forced, imp + FORCE_SCORE, imp)
    val = jnp.where(jb > blk_t, NEG_INF, val)
    n_sel = min(N_SEL, n_blk)
    row8 = lax.broadcasted_iota(jnp.int32, (8, NQB), 0)

    def drop_mask(nb):
        def fn(v):
            if nb <= n_sel:
                return jnp.zeros((n_blk, NQB), F32)
            vals = [v[8 * k:8 * (k + 1)] for k in range(nb // 8)]
            ranks = [jnp.zeros((8, NQB), jnp.int32) for _ in vals]
            for j in range(nb):
                r = jnp.broadcast_to(v[j:j + 1], (8, NQB))
                for k in range(nb // 8):
                    if 8 * k > j:
                        beats = r >= vals[k]
                    elif 8 * k + 7 <= j:
                        beats = r > vals[k]
                    else:
                        beats = (r > vals[k]) | ((r == vals[k]) & (row8 + 8 * k > j))
                    ranks[k] = ranks[k] + jnp.where(beats, 1, 0)
            out = [jnp.where(rk < n_sel, 0.0, NEG_INF) for rk in ranks]
            return jnp.concatenate(out + [jnp.zeros((n_blk - nb, NQB), F32)] * (nb < n_blk), axis=0)
        return fn

    visible = (NQB // SEL_BLOCK) * (qi + 1)
    prefixes = list(range(n_sel, n_blk, n_sel)) + [n_blk]
    choose = drop_mask(prefixes[-1])
    for nb in reversed(prefixes[:-1]):
        choose = functools.partial(
            lambda v, nb, rest: lax.cond(visible <= nb, drop_mask(nb), rest, v), nb=nb, rest=choose)
    drop = choose(val)
    drop = jnp.concatenate([drop] + [jnp.zeros((128 - n_blk, NQB), F32)] * (n_blk < 128), axis=0).astype(BF16)
    rhs_scr[...] = jnp.concatenate([jnp.concatenate([drop] * grp, axis=1), qst,
                                    jnp.zeros((128 - HEAD_DIM, lanes), BF16)], axis=0)

    tk = SEL_TILE
    last_tile = kext_scr.shape[0] // tk - 1

    def logits(i):
        k0 = pl.multiple_of(jnp.minimum(i, last_tile) * tk, tk)
        s = _dot(kext_scr[pl.ds(k0, tk), :], rhs_scr[...])
        d = Q_PER_K * qi - (tk // KT) * i
        entries = [jnp.clip(d - c, -Q_PER_K, N_NEAR) + Q_PER_K for c in range(tk // KT)]
        s = s + jnp.concatenate([sb_ref[0, e] for e in entries], axis=0)
        return s, s.max(axis=0, keepdims=True)

    def softmax(s, mt, m_prev):
        m_new = jnp.maximum(m_prev, mt)
        return jnp.exp(s - m_new).astype(BF16), jnp.exp(m_prev - m_new), m_new

    def pv(i, p, alpha, acc):
        k0 = pl.multiple_of(jnp.clip(i, 0, last_tile) * tk, tk)
        vx = jnp.concatenate([vst_ref[0, :, pl.ds(k0, tk)], ones], axis=0)
        return acc * alpha + _dot(vx, p)

    s0, mt0 = logits(0)
    s1, mt1 = logits(1)
    p0, a0, m0 = softmax(s0, mt0, jnp.full((1, lanes), NEG_INF, F32))
    s_scr[1], mt_scr[1] = s1, mt1
    p_scr[0], al_scr[0] = p0, a0
    m_scr[...] = m0
    acc_scr[...] = jnp.zeros(acc_scr.shape, F32)

    def sel_body(i, carry):
        par = i & 1
        p_old, a_old, acc = p_scr[par], al_scr[par], acc_scr[...]
        s_cur, mt_cur, m_prev = s_scr[1 - par], mt_scr[1 - par], m_scr[...]
        s_new, mt_new = logits(i)
        acc_scr[...] = pv(i - 2, p_old, a_old, acc)
        p_cur, a_cur, m_new = softmax(s_cur, mt_cur, m_prev)
        p_scr[1 - par], al_scr[1 - par] = p_cur, a_cur
        m_scr[...] = m_new
        s_scr[par], mt_scr[par] = s_new, mt_new
        return carry

    n_tiles = jnp.maximum((KT * Q_PER_K * (qi + 1) + tk - 1) // tk, 2)
    lax.fori_loop(2, n_tiles, sel_body, 0)
    par = n_tiles & 1
    acc = pv(n_tiles - 2, p_scr[par], al_scr[par], acc_scr[...])
    p_last, a_last, _ = softmax(s_scr[1 - par], mt_scr[1 - par], m_scr[...])
    acc = pv(n_tiles - 1, p_last, a_last, acc)
    o_s = acc[:HEAD_DIM] * (1.0 / acc[HEAD_DIM:HEAD_DIM + 1])

    gate = jax.nn.sigmoid(gt_ref[0])
    outs = []
    for g in range(grp):
        sl = slice(NQB * g, NQB * (g + 1))
        outs.append(gate[3 * g:3 * g + 1] * o_c[:, sl] + gate[3 * g + 1:3 * g + 2] * o_s[:, sl]
                    + gate[3 * g + 2:3 * g + 3] * o_w[:, sl])
    o_ref[0] = jnp.concatenate(outs, axis=0).T.astype(o_ref.dtype)


def _nsa_attention(qat, gtt, kc, vct, ks, kw, vst, vwt, et, bct, sbt, wbt, ovt):
    bsz, _, t = qat.shape
    hkv = NSA_KV_HEADS
    n_cmp = kc.shape[2]
    gw = NSA_GROUP * HEAD_DIM
    lanes = NSA_GROUP * NQB
    kv_spec = pl.BlockSpec((1, 1, t, HEAD_DIM), lambda bb, hh, i: (bb, hh, 0, 0))
    vt_spec = pl.BlockSpec((1, HEAD_DIM, t), lambda bb, hh, i: (bb, hh, 0))
    tbl = lambda a: pl.BlockSpec((1,) + a.shape[1:], lambda bb, hh, i: (hh, 0, 0, 0))
    const = lambda a: pl.BlockSpec(a.shape, lambda bb, hh, i: (0, 0))
    return pl.pallas_call(
        _nsa_kernel,
        grid=(bsz, hkv, t // NQB),
        in_specs=[
            pl.BlockSpec((1, gw, NQB), lambda bb, hh, i: (bb, hh, i)),
            pl.BlockSpec((1, GATE_ROWS, NQB), lambda bb, hh, i: (bb, hh, i)),
            pl.BlockSpec((1, 1, n_cmp, HEAD_DIM), lambda bb, hh, i: (bb, hh, 0, 0)),
            pl.BlockSpec((1, 1, HEAD_DIM, n_cmp), lambda bb, hh, i: (bb, hh, 0, 0)),
            kv_spec, kv_spec, vt_spec, vt_spec, const(et),
            tbl(bct), tbl(sbt), tbl(wbt), const(ovt),
        ],
        out_specs=pl.BlockSpec((1, NQB, gw), lambda bb, hh, i: (bb, i, hh)),
        out_shape=jax.ShapeDtypeStruct((bsz, t, NSA_HEADS * HEAD_DIM), BF16),
        scratch_shapes=[pltpu.VMEM((t, 256), BF16),
                        pltpu.VMEM((256, lanes), BF16),
                        pltpu.VMEM((2, SEL_TILE, lanes), F32),
                        pltpu.VMEM((2, 1, lanes), F32),
                        pltpu.VMEM((2, SEL_TILE, lanes), BF16),
                        pltpu.VMEM((2, 1, lanes), F32),
                        pltpu.VMEM((1, lanes), F32),
                        pltpu.VMEM((HEAD_DIM + 16, lanes), F32)],
        compiler_params=pltpu.CompilerParams(
            dimension_semantics=("parallel", "parallel", "arbitrary"), vmem_limit_bytes=VMEM_LIMIT),
        name="nsa_attention",
    )(qat, gtt, kc, vct, ks, kw, vst, vwt, et, bct, sbt, wbt, ovt)


def _swa_kernel(sink_ref, qt_ref, kp_ref, kn_ref, vp_ref, vn_ref, wb_ref, o_ref):
    qi = pl.program_id(2)
    qt = qt_ref[0]
    qst = jnp.concatenate([qt[HEAD_DIM * g:HEAD_DIM * (g + 1)] for g in range(SWA_GROUP)], axis=1)
    k = jnp.concatenate([kp_ref[0, 0], kn_ref[0, 0]], axis=0)
    vt = jnp.concatenate([vp_ref[0], vn_ref[0]], axis=1)
    s = _dot(k, qst) + wb_ref[0, jnp.where(qi == 0, 1, 0)]
    sink = sink_ref[0]
    m = jnp.maximum(s.max(axis=0, keepdims=True), sink)
    p = jnp.exp(s - m).astype(BF16)
    vx = jnp.concatenate([vt, jnp.ones((16, 2 * KT), BF16)], axis=0)
    acc = _dot(vx, p)
    o = acc[:HEAD_DIM] * (1.0 / (acc[HEAD_DIM:HEAD_DIM + 1] + jnp.exp(sink - m)))
    o = jnp.concatenate([o[:, QB * g:QB * (g + 1)] for g in range(SWA_GROUP)], axis=0)
    o_ref[0] = o.T.astype(o_ref.dtype)


def _swa_attention(sink_rows, qbt, kb, vbt, wbb):
    bsz, _, t = qbt.shape
    gw = SWA_GROUP * HEAD_DIM
    lanes = SWA_GROUP * QB
    kprev = pl.BlockSpec((1, 1, KT, HEAD_DIM), lambda bb, hh, i: (bb, hh, jnp.maximum(i - 1, 0), 0))
    kcur = pl.BlockSpec((1, 1, KT, HEAD_DIM), lambda bb, hh, i: (bb, hh, i, 0))
    vprev = pl.BlockSpec((1, HEAD_DIM, KT), lambda bb, hh, i: (bb, hh, jnp.maximum(i - 1, 0)))
    vcur = pl.BlockSpec((1, HEAD_DIM, KT), lambda bb, hh, i: (bb, hh, i))
    return pl.pallas_call(
        _swa_kernel,
        grid=(bsz, SWA_KV_HEADS, t // QB),
        in_specs=[
            pl.BlockSpec((1, 1, lanes), lambda bb, hh, i: (hh, 0, 0)),
            pl.BlockSpec((1, gw, QB), lambda bb, hh, i: (bb, hh, i)),
            kprev, kcur, vprev, vcur,
            pl.BlockSpec((1,) + wbb.shape[1:], lambda bb, hh, i: (hh, 0, 0, 0)),
        ],
        out_specs=pl.BlockSpec((1, QB, gw), lambda bb, hh, i: (bb, i, hh)),
        out_shape=jax.ShapeDtypeStruct((bsz, t, SWA_HEADS * HEAD_DIM), BF16),
        compiler_params=pltpu.CompilerParams(
            dimension_semantics=("parallel", "parallel", "arbitrary"), vmem_limit_bytes=VMEM_LIMIT),
        name="swa_attention",
    )(sink_rows, qbt, kb, kb, vbt, vbt, wbb)


def _outproj_kernel(x_ref, oa_ref, ob_ref, w_ref, o_ref):
    na = oa_ref.shape[1]
    o_ref[...] = x_ref[...] + _dot(oa_ref[...], w_ref[:na, :]) + _dot(ob_ref[...], w_ref[na:, :])


def _outproj(x, oa, ob, w, layer):
    n, d = x.shape
    tm = min(ROW_TILE, n)
    row = lambda width: pl.BlockSpec((tm, width), lambda i: (i, 0))
    return pl.pallas_call(
        _outproj_kernel,
        grid=(n // tm,),
        in_specs=[row(d), row(oa.shape[1]), row(ob.shape[1]),
                  pl.BlockSpec((None,) + w.shape[1:], lambda i: (layer, 0, 0))],
        out_specs=row(d),
        out_shape=jax.ShapeDtypeStruct((n, d), F32),
        compiler_params=pltpu.CompilerParams(
            dimension_semantics=("parallel",), vmem_limit_bytes=VMEM_LIMIT),
        name="outproj",
    )(x, oa, ob, w)


def _ffn_kernel(x_ref, g_ref, wg_ref, wu_ref, wd_ref, gf_ref, o_ref, h_scr, acc_scr, *, final_norm):
    f = pl.program_id(1)

    @pl.when(f == 0)
    def _():
        h_scr[...] = _rms(x_ref[...], g_ref[...]).astype(BF16)
        acc_scr[...] = jnp.zeros(acc_scr.shape, F32)

    h = h_scr[...]
    gate = _dot(h, wg_ref[...])
    act = (gate * jax.nn.sigmoid(gate) * _dot(h, wu_ref[...])).astype(BF16)
    acc_scr[...] += _dot(act, wd_ref[...])

    @pl.when(f == pl.num_programs(1) - 1)
    def _():
        y = x_ref[...] + acc_scr[...]
        o_ref[...] = _rms(y, gf_ref[...]) if final_norm else y


def _ffn(x, g, wg, wu, wd, g_final, final_norm, layer):
    n, d = x.shape
    dff = wg.shape[2]
    tm = min(ROW_TILE, n)
    return pl.pallas_call(
        functools.partial(_ffn_kernel, final_norm=final_norm),
        grid=(n // tm, dff // FF_TILE),
        in_specs=[
            pl.BlockSpec((tm, d), lambda i, f: (i, 0)),
            pl.BlockSpec((1, d), lambda i, f: (0, 0)),
            pl.BlockSpec((None, d, FF_TILE), lambda i, f: (layer, 0, f)),
            pl.BlockSpec((None, d, FF_TILE), lambda i, f: (layer, 0, f)),
            pl.BlockSpec((None, FF_TILE, d), lambda i, f: (layer, f, 0)),
            pl.BlockSpec((1, d), lambda i, f: (0, 0)),
        ],
        out_specs=pl.BlockSpec((tm, d), lambda i, f: (i, 0)),
        out_shape=jax.ShapeDtypeStruct((n, d), F32),
        scratch_shapes=[pltpu.VMEM((tm, d), BF16), pltpu.VMEM((tm, d), F32)],
        compiler_params=pltpu.CompilerParams(
            dimension_semantics=("parallel", "arbitrary"), vmem_limit_bytes=VMEM_LIMIT),
        name="ffn",
    )(x, g, wg, wu, wd, g_final)


def _gate_rows(a):
    depth, _, n = a.shape
    a = a.reshape(depth, NSA_KV_HEADS, 3 * NSA_GROUP, n)
    a = jnp.pad(a, ((0, 0), (0, 0), (0, GATE_ROWS - 3 * NSA_GROUP), (0, 0)))
    return a.reshape(depth, N_GATES, n)


def _overlap_t(n_cmp, n_blk):
    ci = jnp.arange(n_cmp)[None, :] * CMP_STRIDE
    sj = jnp.arange(128)[:, None] * SEL_BLOCK
    ov = (ci < sj + SEL_BLOCK) & (ci + CMP_LEN > sj) & (jnp.arange(128)[:, None] < n_blk)
    return (ov & (jnp.arange(n_cmp)[None, :] < n_cmp - 1)).astype(BF16)


def kernel(x, rel_bias, norm_mix, norm_ffn, w_in, b_in, cmp_pos_k, cmp_pos_v, cmp_k_w1, cmp_k_w2,
           cmp_v_w1, cmp_v_w2, sinks, w_out, w_gate, w_up, w_down, norm_final):
    bsz, t, d = x.shape
    depth = w_in.shape[0]
    n_cmp = t // CMP_STRIDE
    sel_idx, win_idx, cmp_idx, swa_idx = _bias_indices(n_cmp)
    tbl_a, tbl_b = rel_bias[:, :NSA_HEADS], rel_bias[:, NSA_HEADS:]
    sbt = _expand_tables(tbl_a, sel_idx, NSA_GROUP)
    wbt = _expand_tables(tbl_a, win_idx, NSA_GROUP)
    bct = _expand_tables(tbl_a, cmp_idx, NSA_GROUP)
    wbb = _expand_tables(tbl_b, swa_idx, SWA_GROUP)
    ovt = _overlap_t(n_cmp, t // SEL_BLOCK)
    et = (jnp.arange(t)[:, None] // SEL_BLOCK == jnp.arange(128)[None, :]).astype(BF16)
    w_in_t = jnp.swapaxes(w_in.astype(BF16), -1, -2)
    b_row = jnp.concatenate([b_in[:, a:b] for a, b in ROWMAJOR], axis=-1).astype(F32)
    w_gt, b_gt = _gate_rows(w_in_t[:, COL_GT[0]:COL_GT[1], :]), _gate_rows(b_in[:, COL_GT[0]:COL_GT[1], None])
    w_out16, w_gate16, w_up16, w_down16 = (w.astype(BF16) for w in (w_out, w_gate, w_up, w_down))
    for layer in range(depth):
        qat, kc_raw, vc_raw, ks, kw, vst, vwt, gtt, qbt, kb, vbt = _inproj(
            x, norm_mix[layer][None, :], w_in_t, b_row[layer][None, :],
            b_in[layer].astype(F32)[:, None], w_gt[layer], b_gt[layer].astype(F32), layer)
        kc, vct = _compress(kc_raw, vc_raw, cmp_pos_k[layer], cmp_pos_v[layer],
                            cmp_k_w1[layer].astype(BF16), cmp_k_w2[layer].astype(BF16),
                            cmp_v_w1[layer].astype(BF16), cmp_v_w2[layer].T.astype(BF16))
        o_a = _nsa_attention(qat, gtt, kc, vct, ks, kw, vst, vwt, et, bct, sbt, wbt, ovt)
        sink_rows = jnp.repeat(sinks[layer].astype(F32), QB).reshape(SWA_KV_HEADS, 1, SWA_GROUP * QB)
        o_b = _swa_attention(sink_rows, qbt, kb, vbt, wbb)
        x2 = _outproj(x.reshape(bsz * t, d), o_a.reshape(bsz * t, -1), o_b.reshape(bsz * t, -1),
                      w_out16, layer)
        x2 = _ffn(x2, norm_ffn[layer][None, :], w_gate16, w_up16, w_down16, norm_final[None, :],
                  layer == depth - 1, layer)
        x = x2.reshape(bsz, t, d)
    return x
```

```python
import functools
import math

import jax
import jax.numpy as jnp
from jax import lax
from jax.experimental import pallas as pl
from jax.experimental.pallas import tpu as pltpu

F32 = jnp.float32
BF16 = jnp.bfloat16

D_MODEL = 2048
HEAD_DIM = 64
NSA_HEADS = 16
NSA_KV_HEADS = 4
NSA_GROUP = 4
SWA_HEADS = 16
SWA_KV_HEADS = 2
SWA_GROUP = 8
CMP_LEN = 32
CMP_STRIDE = 16
CMP_HIDDEN = 256
SEL_BLOCK = 64
SEL_SHIFT = 6
N_SEL = 16
NSA_WINDOW = 512
SWA_WINDOW = 128
NUM_BUCKETS = 32
REL_MAX_DISTANCE = 1024
RMS_EPS = 1e-5
NEG_INF = -1e30
FORCE_SCORE = 1e6

QB = 128
NQB = 256
KT = 128
SEL_TILE = 4 * KT
Q_PER_K = NQB // KT
C_PER_Q = NQB // CMP_STRIDE
N_NEAR = 8
N_WIN_TILES = NSA_WINDOW // KT + 1
MASKED = NUM_BUCKETS
ROW_TILE = 512
FF_TILE = 512
VMEM_LIMIT = 56 * 1024 * 1024

COL_QA = (0, 1024)
COL_KC, COL_VC = (1024, 1280), (1280, 1536)
COL_KS, COL_VS = (1536, 1792), (1792, 2048)
COL_KW, COL_VW = (2048, 2304), (2304, 2560)
COL_GT = (2560, 2608)
COL_QB = (2608, 3632)
COL_KB, COL_VB = (3632, 3760), (3760, 3888)
GATE_ROWS = 16
N_GATES = NSA_KV_HEADS * GATE_ROWS
ROWMAJOR = (COL_KC, COL_VC, COL_KS, COL_KW, COL_KB)


def _dot(a, b):
    return jnp.dot(a, b, preferred_element_type=F32)


def _dot_nt(a, b):
    return lax.dot_general(a, b, (((1,), (1,)), ((), ())), preferred_element_type=F32)


def _rms(x, g):
    return x * lax.rsqrt(jnp.mean(x * x, axis=-1, keepdims=True) + RMS_EPS) * g


def _inproj_kernel(x_ref, g_ref, wt_ref, br_ref, bc_ref, wg_ref, bg_ref,
                   qat_ref, kc_ref, vc_ref, ks_ref, kw_ref, vst_ref, vwt_ref, gtt_ref,
                   qbt_ref, kb_ref, vbt_ref):
    h = _rms(x_ref[0], g_ref[...]).astype(BF16)

    def colmajor(cols):
        a, b = cols
        return _dot_nt(wt_ref[a:b, :], h) + bc_ref[a:b, :]

    off = 0
    for (a, b), ref in zip(ROWMAJOR, (kc_ref, vc_ref, ks_ref, kw_ref, kb_ref)):
        r = _dot_nt(h, wt_ref[a:b, :]) + br_ref[:, off:off + b - a]
        off += b - a
        for hh in range((b - a) // HEAD_DIM):
            ref[0, hh] = r[:, HEAD_DIM * hh:HEAD_DIM * (hh + 1)].astype(ref.dtype)
    qat_ref[0] = (colmajor(COL_QA) * (HEAD_DIM ** -0.5)).astype(qat_ref.dtype)
    qbt_ref[0] = (colmajor(COL_QB) * (HEAD_DIM ** -0.5)).astype(qbt_ref.dtype)
    vst_ref[0] = colmajor(COL_VS).astype(vst_ref.dtype)
    vwt_ref[0] = colmajor(COL_VW).astype(vwt_ref.dtype)
    vbt_ref[0] = colmajor(COL_VB).astype(vbt_ref.dtype)
    gtt_ref[0] = _dot_nt(wg_ref[...], h) + bg_ref[...]


def _inproj(x, g, wt, b_row, b_col, wg, bg, layer):
    bsz, t, d = x.shape
    tm = min(ROW_TILE, t)
    kv4 = lambda dt: jax.ShapeDtypeStruct((bsz, NSA_KV_HEADS, t, HEAD_DIM), dt)
    kv2 = jax.ShapeDtypeStruct((bsz, SWA_KV_HEADS, t, HEAD_DIM), BF16)
    colm = lambda rows, dt: jax.ShapeDtypeStruct((bsz, rows, t), dt)
    out_shape = (colm(1024, BF16), kv4(F32), kv4(F32), kv4(BF16), kv4(BF16),
                 colm(256, BF16), colm(256, BF16), colm(N_GATES, F32),
                 colm(1024, BF16), kv2, colm(128, BF16))
    kv4_spec = pl.BlockSpec((1, NSA_KV_HEADS, tm, HEAD_DIM), lambda bb, i: (bb, 0, i, 0))
    kv2_spec = pl.BlockSpec((1, SWA_KV_HEADS, tm, HEAD_DIM), lambda bb, i: (bb, 0, i, 0))
    colm_spec = lambda rows: pl.BlockSpec((1, rows, tm), lambda bb, i: (bb, 0, i))
    const = lambda a: pl.BlockSpec(a.shape, lambda bb, i: (0, 0), pipeline_mode=pl.Buffered(1))
    return pl.pallas_call(
        _inproj_kernel,
        grid=(bsz, t // tm),
        in_specs=[pl.BlockSpec((1, tm, d), lambda bb, i: (bb, i, 0)),
                  const(g),
                  pl.BlockSpec((None,) + wt.shape[1:], lambda bb, i: (layer, 0, 0), pipeline_mode=pl.Buffered(1)),
                  const(b_row), const(b_col), const(wg), const(bg)],
        out_specs=(colm_spec(1024), kv4_spec, kv4_spec, kv4_spec, kv4_spec,
                   colm_spec(256), colm_spec(256), colm_spec(N_GATES),
                   colm_spec(1024), kv2_spec, colm_spec(128)),
        out_shape=out_shape,
        compiler_params=pltpu.CompilerParams(
            dimension_semantics=("parallel", "parallel"), vmem_limit_bytes=VMEM_LIMIT),
        name="inproj",
    )(x, g, wt, b_row, b_col, wg, bg)


def _compress_kernel(kr_ref, vr_ref, pk_ref, pv_ref, k1_ref, k2_ref, v1_ref, v2t_ref,
                     ko_ref, vot_ref):
    n_chunk = kr_ref.shape[2] // CMP_STRIDE
    half = CMP_STRIDE * HEAD_DIM

    def hidden(r_ref, p_ref, w1_ref):
        top = jnp.zeros((n_chunk, CMP_HIDDEN), F32)
        bot = jnp.zeros((n_chunk, CMP_HIDDEN), F32)
        for l in range(CMP_STRIDE):
            xl = r_ref[0, 0, pl.ds(l, n_chunk, stride=CMP_STRIDE), :]
            a = (xl + p_ref[l:l + 1, :]).astype(BF16)
            top = top + _dot(a, w1_ref[HEAD_DIM * l:HEAD_DIM * (l + 1), :])
            c = (xl + p_ref[CMP_STRIDE + l:CMP_STRIDE + l + 1, :]).astype(BF16)
            bot = bot + _dot(c, w1_ref[half + HEAD_DIM * l:half + HEAD_DIM * (l + 1), :])
        hid = top + pltpu.roll(bot, n_chunk - 1, axis=0)
        return jax.nn.gelu(hid).astype(BF16)

    out = _dot(hidden(kr_ref, pk_ref, k1_ref), k2_ref[...])
    row = lax.broadcasted_iota(jnp.int32, out.shape, 0)
    ko_ref[0, 0] = jnp.where(row < n_chunk - 1, out, 0.0).astype(ko_ref.dtype)
    out_t = _dot_nt(v2t_ref[...], hidden(vr_ref, pv_ref, v1_ref))
    col = lax.broadcasted_iota(jnp.int32, out_t.shape, 1)
    vot_ref[0, 0] = jnp.where(col < n_chunk - 1, out_t, 0.0).astype(vot_ref.dtype)


def _compress(kc_raw, vc_raw, pos_k, pos_v, k1, k2, v1, v2t):
    bsz, hkv, t, dh = kc_raw.shape
    n_chunk = t // CMP_STRIDE
    raw_spec = pl.BlockSpec((1, 1, t, dh), lambda bb, hh: (bb, hh, 0, 0))
    full = lambda a: pl.BlockSpec(a.shape, lambda bb, hh: (0,) * a.ndim)
    return pl.pallas_call(
        _compress_kernel,
        grid=(bsz, hkv),
        in_specs=[raw_spec, raw_spec, full(pos_k), full(pos_v), full(k1), full(k2), full(v1), full(v2t)],
        out_specs=(pl.BlockSpec((1, 1, n_chunk, dh), lambda bb, hh: (bb, hh, 0, 0)),
                   pl.BlockSpec((1, 1, dh, n_chunk), lambda bb, hh: (bb, hh, 0, 0))),
        out_shape=(jax.ShapeDtypeStruct((bsz, hkv, n_chunk, dh), BF16),
                   jax.ShapeDtypeStruct((bsz, hkv, dh, n_chunk), BF16)),
        compiler_params=pltpu.CompilerParams(
            dimension_semantics=("parallel", "parallel"), vmem_limit_bytes=VMEM_LIMIT),
        name="compress",
    )(kc_raw, vc_raw, pos_k, pos_v, k1, k2, v1, v2t)


def _expand_kernel(tbl_ref, idx_ref, o_ref, *, gsz):
    grp = pl.program_id(0)
    idx = idx_ref[0]
    w = idx.shape[1]
    accs = [jnp.full(idx.shape, tbl_ref[MASKED, grp * gsz + g], F32) for g in range(gsz)]
    for b in range(NUM_BUCKETS):
        hit = idx == b
        for g in range(gsz):
            accs[g] = jnp.where(hit, tbl_ref[b, grp * gsz + g], accs[g])
    for g in range(gsz):
        o_ref[0, 0, :, w * g:w * (g + 1)] = accs[g]


def _expand_tables(tbl, idx, gsz):
    r0, r1, w = idx.shape
    heads = tbl.shape[1]
    tr = min(r1, 128)
    tbl_ext = jnp.concatenate([tbl.astype(F32), jnp.full((1, heads), NEG_INF, F32)], axis=0)
    return pl.pallas_call(
        functools.partial(_expand_kernel, gsz=gsz),
        grid=(heads // gsz, r0, r1 // tr),
        in_specs=[pl.BlockSpec(memory_space=pltpu.SMEM),
                  pl.BlockSpec((1, tr, w), lambda gg, a, r: (a, r, 0))],
        out_specs=pl.BlockSpec((1, 1, tr, gsz * w), lambda gg, a, r: (gg, a, r, 0)),
        out_shape=jax.ShapeDtypeStruct((heads // gsz, r0, r1, gsz * w), F32),
        compiler_params=pltpu.CompilerParams(
            dimension_semantics=("parallel", "parallel", "parallel"), vmem_limit_bytes=VMEM_LIMIT),
        name="expand_bias",
    )(tbl_ext, idx)


def _t5_bucket(dist):
    dist = jnp.maximum(dist, 0)
    max_exact = NUM_BUCKETS // 2
    d = jnp.maximum(dist, 1).astype(F32)
    ratio = jnp.log(d / max_exact) / math.log(REL_MAX_DISTANCE / max_exact)
    large = max_exact + (ratio * (NUM_BUCKETS - max_exact)).astype(jnp.int32)
    large = jnp.minimum(large, NUM_BUCKETS - 1)
    return jnp.where(dist < max_exact, dist, large)


def _bias_indices(n_cmp):
    kj = jnp.arange(KT)[:, None]
    qq = jnp.arange(NQB)[None, :]
    d = jnp.arange(-Q_PER_K, N_NEAR + 1)[:, None, None]
    dist = KT * d + qq - kj
    sel_idx = jnp.where(dist >= 0, _t5_bucket(dist), MASKED)
    d = jnp.arange(-(Q_PER_K - 1), N_WIN_TILES + 1)[:, None, None]
    dist = KT * d + qq - kj
    win_idx = jnp.where((dist >= 0) & (dist < NSA_WINDOW), _t5_bucket(dist), MASKED)
    cprime = jnp.arange(2 * n_cmp)[:, None] - (n_cmp - C_PER_Q)
    dist = qq - CMP_STRIDE * cprime - (CMP_LEN - 1)
    cmp_idx = jnp.where(dist >= 0, _t5_bucket(dist), MASKED)[None]
    kr = jnp.arange(2 * KT)[:, None]
    db = jnp.arange(QB)[None, :] + KT - kr
    ok = (db >= 0) & (db < SWA_WINDOW)
    ok = jnp.stack([ok, ok & (kr >= KT)])
    swa_idx = jnp.where(ok, _t5_bucket(db)[None], MASKED)
    i32 = lambda a: a.astype(jnp.int32)
    return i32(sel_idx), i32(win_idx), i32(cmp_idx), i32(swa_idx)


def _softmax_pv(s_tiles, vt_tiles, ones):
    m = s_tiles[0].max(axis=0, keepdims=True)
    for s in s_tiles[1:]:
        m = jnp.maximum(m, s.max(axis=0, keepdims=True))
    acc = None
    for s, vt in zip(s_tiles, vt_tiles):
        p = jnp.exp(s - m).astype(BF16)
        vx = jnp.concatenate([vt, ones[:, :vt.shape[1]]], axis=0)
        part = _dot(vx, p)
        acc = part if acc is None else acc + part
    return acc[:HEAD_DIM] * (1.0 / acc[HEAD_DIM:HEAD_DIM + 1])


def _nsa_kernel(qt_ref, gt_ref, kc_ref, vct_ref, ks_ref, kw_ref, vst_ref, vwt_ref, et_ref,
                bc_ref, sb_ref, wb_ref, ov_ref, o_ref,
                kext_scr, rhs_scr, s_scr, mt_scr, p_scr, al_scr, m_scr, acc_scr):
    qi = pl.program_id(2)
    t0 = qi * NQB
    n_cmp = kc_ref.shape[2]
    n_blk = ks_ref.shape[2] // SEL_BLOCK
    grp = NSA_GROUP
    lanes = grp * NQB

    @pl.when(qi == 0)
    def _():
        kext_scr[:, 0:128] = et_ref[...]
        kext_scr[:, 128:128 + HEAD_DIM] = ks_ref[0, 0]
        kext_scr[:, 128 + HEAD_DIM:] = jnp.zeros((kext_scr.shape[0], 128 - HEAD_DIM), BF16)

    qt = qt_ref[0]
    qst = jnp.concatenate([qt[HEAD_DIM * g:HEAD_DIM * (g + 1)] for g in range(grp)], axis=1)
    ones = jnp.ones((16, SEL_TILE), BF16)

    start = pl.multiple_of(n_cmp - C_PER_Q - C_PER_Q * qi, 8)
    s = _dot(kc_ref[0, 0], qst) + bc_ref[0, 0, pl.ds(start, n_cmp), :]
    e = jnp.exp(s - s.max(axis=0, keepdims=True))
    tq = t0 + (lax.broadcasted_iota(jnp.int32, (1, lanes), 1) & (NQB - 1))
    inv = jnp.where(tq >= CMP_LEN - 1, 1.0 / e.sum(axis=0, keepdims=True), 0.0)
    p = e * inv
    o_c = _dot(vct_ref[0, 0], p.astype(BF16))

    s_tiles, vt_tiles = [], []
    n_win_entries = Q_PER_K - 1 + N_WIN_TILES
    for d in range(-(Q_PER_K - 1), N_WIN_TILES):
        jj = Q_PER_K * qi - d
        k0 = pl.multiple_of(jnp.maximum(jj, 0) * KT, KT)
        entry = jnp.where(jj >= 0, d + Q_PER_K - 1, n_win_entries)
        s_tiles.append(_dot(kw_ref[0, 0, pl.ds(k0, KT), :], qst) + wb_ref[0, entry])
        vt_tiles.append(vwt_ref[0, :, pl.ds(k0, KT)])
    o_w = _softmax_pv(s_tiles, vt_tiles, ones)

    p_sum = p[:, 0:NQB]
    for g in range(1, grp):
        p_sum = p_sum + p[:, NQB * g:NQB * (g + 1)]
    p_hi = p_sum.astype(BF16)
    p_lo = (p_sum - p_hi.astype(F32)).astype(BF16)
    imp = (_dot(ov_ref[...], p_hi) + _dot(ov_ref[...], p_lo))[:n_blk]
    jb = lax.broadcasted_iota(jnp.int32, (n_blk, NQB), 0)
    blk_t = (t0 + lax.broadcasted_iota(jnp.int32, (n_blk, NQB), 1)) >> SEL_SHIFT
    forced = (jb == 0) | (jb == blk_t) | (jb == blk_t - 1)
    val = jnp.where(forced, imp + FORCE_SCORE, imp)
    val = jnp.where(jb > blk_t, NEG_INF, val)
    n_sel = min(N_SEL, n_blk)
    row8 = lax.broadcasted_iota(jnp.int32, (8, NQB), 0)

    def drop_mask(nb):
        def fn(v):
            if nb <= n_sel:
                return jnp.zeros((n_blk, NQB), F32)
            vals = [v[8 * k:8 * (k + 1)] for k in range(nb // 8)]
            ranks = [jnp.zeros((8, NQB), jnp.int32) for _ in vals]
            for j in range(nb):
                r = jnp.broadcast_to(v[j:j + 1], (8, NQB))
                for k in range(nb // 8):
                    if 8 * k > j:
                        beats = r >= vals[k]
                    elif 8 * k + 7 <= j:
                        beats = r > vals[k]
                    else:
                        beats = (r > vals[k]) | ((r == vals[k]) & (row8 + 8 * k > j))
                    ranks[k] = ranks[k] + jnp.where(beats, 1, 0)
            out = [jnp.where(rk < n_sel, 0.0, NEG_INF) for rk in ranks]
            return jnp.concatenate(out + [jnp.zeros((n_blk - nb, NQB), F32)] * (nb < n_blk), axis=0)
        return fn

    visible = (NQB // SEL_BLOCK) * (qi + 1)
    prefixes = list(range(n_sel, n_blk, n_sel)) + [n_blk]
    choose = drop_mask(prefixes[-1])
    for nb in reversed(prefixes[:-1]):
        choose = functools.partial(
            lambda v, nb, rest: lax.cond(visible <= nb, drop_mask(nb), rest, v), nb=nb, rest=choose)
    drop = choose(val)
    drop = jnp.concatenate([drop] + [jnp.zeros((128 - n_blk, NQB), F32)] * (n_blk < 128), axis=0).astype(BF16)
    rhs_scr[...] = jnp.concatenate([jnp.concatenate([drop] * grp, axis=1), qst,
                                    jnp.zeros((128 - HEAD_DIM, lanes), BF16)], axis=0)

    tk = SEL_TILE
    last_tile = kext_scr.shape[0] // tk - 1

    def qk(i):
        k0 = pl.multiple_of(jnp.minimum(i, last_tile) * tk, tk)
        return _dot(kext_scr[pl.ds(k0, tk), :], rhs_scr[...])

    def biased(s, i):
        d = Q_PER_K * qi - (tk // KT) * i
        entries = [jnp.clip(d - c, -Q_PER_K, N_NEAR) + Q_PER_K for c in range(tk // KT)]
        s = s + jnp.concatenate([sb_ref[0, e] for e in entries], axis=0)
        return s, s.max(axis=0, keepdims=True)

    def logits(i):
        return biased(qk(i), i)

    def softmax(s, mt, m_prev):
        m_new = jnp.maximum(m_prev, mt)
        return jnp.exp(s - m_new).astype(BF16), jnp.exp(m_prev - m_new), m_new

    def pv_raw(i, p):
        k0 = pl.multiple_of(jnp.clip(i, 0, last_tile) * tk, tk)
        vx = jnp.concatenate([vst_ref[0, :, pl.ds(k0, tk)], ones], axis=0)
        return _dot(vx, p)

    def pv(i, p, alpha, acc):
        return acc * alpha + pv_raw(i, p)

    r0, r1 = qk(0), qk(1)
    s0, mt0 = biased(r0, 0)
    p0, a0, m0 = softmax(s0, mt0, jnp.full((1, lanes), NEG_INF, F32))
    s_scr[1], mt_scr[1] = biased(r1, 1)
    p_scr[0], al_scr[0] = p0, a0
    m_scr[...] = m0
    acc_scr[...] = jnp.zeros(acc_scr.shape, F32)

    def sel_body(i, carry):
        par = i & 1
        p_old, a_old, acc = p_scr[par], al_scr[par], acc_scr[...]
        s_cur, mt_cur, m_prev = s_scr[1 - par], mt_scr[1 - par], m_scr[...]
        s_raw = qk(i)
        o_raw = pv_raw(i - 2, p_old)
        p_cur, a_cur, m_new = softmax(s_cur, mt_cur, m_prev)
        p_scr[1 - par], al_scr[1 - par] = p_cur, a_cur
        m_scr[...] = m_new
        s_scr[par], mt_scr[par] = biased(s_raw, i)
        acc_scr[...] = acc * a_old + o_raw
        return carry

    n_tiles = jnp.maximum((KT * Q_PER_K * (qi + 1) + tk - 1) // tk, 2)
    lax.fori_loop(2, n_tiles, sel_body, 0)
    par = n_tiles & 1
    o_raw = pv_raw(n_tiles - 2, p_scr[par])
    p_last, a_last, _ = softmax(s_scr[1 - par], mt_scr[1 - par], m_scr[...])
    acc = pv(n_tiles - 1, p_last, a_last, acc_scr[...] * al_scr[par] + o_raw)
    o_s = acc[:HEAD_DIM] * (1.0 / acc[HEAD_DIM:HEAD_DIM + 1])

    gate = jax.nn.sigmoid(gt_ref[0])
    outs = []
    for g in range(grp):
        sl = slice(NQB * g, NQB * (g + 1))
        outs.append(gate[3 * g:3 * g + 1] * o_c[:, sl] + gate[3 * g + 1:3 * g + 2] * o_s[:, sl]
                    + gate[3 * g + 2:3 * g + 3] * o_w[:, sl])
    o_ref[0] = jnp.concatenate(outs, axis=0).T.astype(o_ref.dtype)


def _nsa_attention(qat, gtt, kc, vct, ks, kw, vst, vwt, et, bct, sbt, wbt, ovt):
    bsz, _, t = qat.shape
    hkv = NSA_KV_HEADS
    n_cmp = kc.shape[2]
    gw = NSA_GROUP * HEAD_DIM
    lanes = NSA_GROUP * NQB
    kv_spec = pl.BlockSpec((1, 1, t, HEAD_DIM), lambda bb, hh, i: (bb, hh, 0, 0))
    vt_spec = pl.BlockSpec((1, HEAD_DIM, t), lambda bb, hh, i: (bb, hh, 0))
    tbl = lambda a: pl.BlockSpec((1,) + a.shape[1:], lambda bb, hh, i: (hh, 0, 0, 0))
    const = lambda a: pl.BlockSpec(a.shape, lambda bb, hh, i: (0, 0))
    return pl.pallas_call(
        _nsa_kernel,
        grid=(bsz, hkv, t // NQB),
        in_specs=[
            pl.BlockSpec((1, gw, NQB), lambda bb, hh, i: (bb, hh, i)),
            pl.BlockSpec((1, GATE_ROWS, NQB), lambda bb, hh, i: (bb, hh, i)),
            pl.BlockSpec((1, 1, n_cmp, HEAD_DIM), lambda bb, hh, i: (bb, hh, 0, 0)),
            pl.BlockSpec((1, 1, HEAD_DIM, n_cmp), lambda bb, hh, i: (bb, hh, 0, 0)),
            kv_spec, kv_spec, vt_spec, vt_spec, const(et),
            tbl(bct), tbl(sbt), tbl(wbt), const(ovt),
        ],
        out_specs=pl.BlockSpec((1, NQB, gw), lambda bb, hh, i: (bb, i, hh)),
        out_shape=jax.ShapeDtypeStruct((bsz, t, NSA_HEADS * HEAD_DIM), BF16),
        scratch_shapes=[pltpu.VMEM((t, 256), BF16),
                        pltpu.VMEM((256, lanes), BF16),
                        pltpu.VMEM((2, SEL_TILE, lanes), F32),
                        pltpu.VMEM((2, 1, lanes), F32),
                        pltpu.VMEM((2, SEL_TILE, lanes), BF16),
                        pltpu.VMEM((2, 1, lanes), F32),
                        pltpu.VMEM((1, lanes), F32),
                        pltpu.VMEM((HEAD_DIM + 16, lanes), F32)],
        compiler_params=pltpu.CompilerParams(
            dimension_semantics=("parallel", "parallel", "arbitrary"), vmem_limit_bytes=VMEM_LIMIT),
        name="nsa_attention",
    )(qat, gtt, kc, vct, ks, kw, vst, vwt, et, bct, sbt, wbt, ovt)


def _swa_kernel(sink_ref, qt_ref, kp_ref, kn_ref, vp_ref, vn_ref, wb_ref, o_ref):
    qi = pl.program_id(2)
    qt = qt_ref[0]
    qst = jnp.concatenate([qt[HEAD_DIM * g:HEAD_DIM * (g + 1)] for g in range(SWA_GROUP)], axis=1)
    k = jnp.concatenate([kp_ref[0, 0], kn_ref[0, 0]], axis=0)
    vt = jnp.concatenate([vp_ref[0], vn_ref[0]], axis=1)
    s = _dot(k, qst) + wb_ref[0, jnp.where(qi == 0, 1, 0)]
    sink = sink_ref[0]
    m = jnp.maximum(s.max(axis=0, keepdims=True), sink)
    p = jnp.exp(s - m).astype(BF16)
    vx = jnp.concatenate([vt, jnp.ones((16, 2 * KT), BF16)], axis=0)
    acc = _dot(vx, p)
    o = acc[:HEAD_DIM] * (1.0 / (acc[HEAD_DIM:HEAD_DIM + 1] + jnp.exp(sink - m)))
    o = jnp.concatenate([o[:, QB * g:QB * (g + 1)] for g in range(SWA_GROUP)], axis=0)
    o_ref[0] = o.T.astype(o_ref.dtype)


def _swa_attention(sink_rows, qbt, kb, vbt, wbb):
    bsz, _, t = qbt.shape
    gw = SWA_GROUP * HEAD_DIM
    lanes = SWA_GROUP * QB
    kprev = pl.BlockSpec((1, 1, KT, HEAD_DIM), lambda bb, hh, i: (bb, hh, jnp.maximum(i - 1, 0), 0))
    kcur = pl.BlockSpec((1, 1, KT, HEAD_DIM), lambda bb, hh, i: (bb, hh, i, 0))
    vprev = pl.BlockSpec((1, HEAD_DIM, KT), lambda bb, hh, i: (bb, hh, jnp.maximum(i - 1, 0)))
    vcur = pl.BlockSpec((1, HEAD_DIM, KT), lambda bb, hh, i: (bb, hh, i))
    return pl.pallas_call(
        _swa_kernel,
        grid=(bsz, SWA_KV_HEADS, t // QB),
        in_specs=[
            pl.BlockSpec((1, 1, lanes), lambda bb, hh, i: (hh, 0, 0)),
            pl.BlockSpec((1, gw, QB), lambda bb, hh, i: (bb, hh, i)),
            kprev, kcur, vprev, vcur,
            pl.BlockSpec((1,) + wbb.shape[1:], lambda bb, hh, i: (hh, 0, 0, 0)),
        ],
        out_specs=pl.BlockSpec((1, QB, gw), lambda bb, hh, i: (bb, i, hh)),
        out_shape=jax.ShapeDtypeStruct((bsz, t, SWA_HEADS * HEAD_DIM), BF16),
        compiler_params=pltpu.CompilerParams(
            dimension_semantics=("parallel", "parallel", "arbitrary"), vmem_limit_bytes=VMEM_LIMIT),
        name="swa_attention",
    )(sink_rows, qbt, kb, kb, vbt, vbt, wbb)


def _outproj_kernel(x_ref, oa_ref, ob_ref, w_ref, o_ref):
    na = oa_ref.shape[1]
    o_ref[...] = x_ref[...] + _dot(oa_ref[...], w_ref[:na, :]) + _dot(ob_ref[...], w_ref[na:, :])


def _outproj(x, oa, ob, w, layer):
    n, d = x.shape
    tm = min(ROW_TILE, n)
    row = lambda width: pl.BlockSpec((tm, width), lambda i: (i, 0))
    return pl.pallas_call(
        _outproj_kernel,
        grid=(n // tm,),
        in_specs=[row(d), row(oa.shape[1]), row(ob.shape[1]),
                  pl.BlockSpec((None,) + w.shape[1:], lambda i: (layer, 0, 0))],
        out_specs=row(d),
        out_shape=jax.ShapeDtypeStruct((n, d), F32),
        compiler_params=pltpu.CompilerParams(
            dimension_semantics=("parallel",), vmem_limit_bytes=VMEM_LIMIT),
        name="outproj",
    )(x, oa, ob, w)


def _ffn_kernel(x_ref, g_ref, wg_ref, wu_ref, wd_ref, gf_ref, o_ref, h_scr, acc_scr, *, final_norm):
    f = pl.program_id(1)

    @pl.when(f == 0)
    def _():
        h_scr[...] = _rms(x_ref[...], g_ref[...]).astype(BF16)
        acc_scr[...] = jnp.zeros(acc_scr.shape, F32)

    h = h_scr[...]
    gate = _dot(h, wg_ref[...])
    act = (gate * jax.nn.sigmoid(gate) * _dot(h, wu_ref[...])).astype(BF16)
    acc_scr[...] += _dot(act, wd_ref[...])

    @pl.when(f == pl.num_programs(1) - 1)
    def _():
        y = x_ref[...] + acc_scr[...]
        o_ref[...] = _rms(y, gf_ref[...]) if final_norm else y


def _ffn(x, g, wg, wu, wd, g_final, final_norm, layer):
    n, d = x.shape
    dff = wg.shape[2]
    tm = min(ROW_TILE, n)
    return pl.pallas_call(
        functools.partial(_ffn_kernel, final_norm=final_norm),
        grid=(n // tm, dff // FF_TILE),
        in_specs=[
            pl.BlockSpec((tm, d), lambda i, f: (i, 0)),
            pl.BlockSpec((1, d), lambda i, f: (0, 0)),
            pl.BlockSpec((None, d, FF_TILE), lambda i, f: (layer, 0, f)),
            pl.BlockSpec((None, d, FF_TILE), lambda i, f: (layer, 0, f)),
            pl.BlockSpec((None, FF_TILE, d), lambda i, f: (layer, f, 0)),
            pl.BlockSpec((1, d), lambda i, f: (0, 0)),
        ],
        out_specs=pl.BlockSpec((tm, d), lambda i, f: (i, 0)),
        out_shape=jax.ShapeDtypeStruct((n, d), F32),
        scratch_shapes=[pltpu.VMEM((tm, d), BF16), pltpu.VMEM((tm, d), F32)],
        compiler_params=pltpu.CompilerParams(
            dimension_semantics=("parallel", "arbitrary"), vmem_limit_bytes=VMEM_LIMIT),
        name="ffn",
    )(x, g, wg, wu, wd, g_final)


def _gate_rows(a):
    depth, _, n = a.shape
    a = a.reshape(depth, NSA_KV_HEADS, 3 * NSA_GROUP, n)
    a = jnp.pad(a, ((0, 0), (0, 0), (0, GATE_ROWS - 3 * NSA_GROUP), (0, 0)))
    return a.reshape(depth, N_GATES, n)


def _overlap_t(n_cmp, n_blk):
    ci = jnp.arange(n_cmp)[None, :] * CMP_STRIDE
    sj = jnp.arange(128)[:, None] * SEL_BLOCK
    ov = (ci < sj + SEL_BLOCK) & (ci + CMP_LEN > sj) & (jnp.arange(128)[:, None] < n_blk)
    return (ov & (jnp.arange(n_cmp)[None, :] < n_cmp - 1)).astype(BF16)


def kernel(x, rel_bias, norm_mix, norm_ffn, w_in, b_in, cmp_pos_k, cmp_pos_v, cmp_k_w1, cmp_k_w2,
           cmp_v_w1, cmp_v_w2, sinks, w_out, w_gate, w_up, w_down, norm_final):
    bsz, t, d = x.shape
    depth = w_in.shape[0]
    n_cmp = t // CMP_STRIDE
    sel_idx, win_idx, cmp_idx, swa_idx = _bias_indices(n_cmp)
    tbl_a, tbl_b = rel_bias[:, :NSA_HEADS], rel_bias[:, NSA_HEADS:]
    sbt = _expand_tables(tbl_a, sel_idx, NSA_GROUP)
    wbt = _expand_tables(tbl_a, win_idx, NSA_GROUP)
    bct = _expand_tables(tbl_a, cmp_idx, NSA_GROUP)
    wbb = _expand_tables(tbl_b, swa_idx, SWA_GROUP)
    ovt = _overlap_t(n_cmp, t // SEL_BLOCK)
    et = (jnp.arange(t)[:, None] // SEL_BLOCK == jnp.arange(128)[None, :]).astype(BF16)
    w_in_t = jnp.swapaxes(w_in.astype(BF16), -1, -2)
    b_row = jnp.concatenate([b_in[:, a:b] for a, b in ROWMAJOR], axis=-1).astype(F32)
    w_gt, b_gt = _gate_rows(w_in_t[:, COL_GT[0]:COL_GT[1], :]), _gate_rows(b_in[:, COL_GT[0]:COL_GT[1], None])
    w_out16, w_gate16, w_up16, w_down16 = (w.astype(BF16) for w in (w_out, w_gate, w_up, w_down))
    for layer in range(depth):
        qat, kc_raw, vc_raw, ks, kw, vst, vwt, gtt, qbt, kb, vbt = _inproj(
            x, norm_mix[layer][None, :], w_in_t, b_row[layer][None, :],
            b_in[layer].astype(F32)[:, None], w_gt[layer], b_gt[layer].astype(F32), layer)
        kc, vct = _compress(kc_raw, vc_raw, cmp_pos_k[layer], cmp_pos_v[layer],
                            cmp_k_w1[layer].astype(BF16), cmp_k_w2[layer].astype(BF16),
                            cmp_v_w1[layer].astype(BF16), cmp_v_w2[layer].T.astype(BF16))
        o_a = _nsa_attention(qat, gtt, kc, vct, ks, kw, vst, vwt, et, bct, sbt, wbt, ovt)
        sink_rows = jnp.repeat(sinks[layer].astype(F32), QB).reshape(SWA_KV_HEADS, 1, SWA_GROUP * QB)
        o_b = _swa_attention(sink_rows, qbt, kb, vbt, wbb)
        x2 = _outproj(x.reshape(bsz * t, d), o_a.reshape(bsz * t, -1), o_b.reshape(bsz * t, -1),
                      w_out16, layer)
        x2 = _ffn(x2, norm_ffn[layer][None, :], w_gate16, w_up16, w_down16, norm_final[None, :],
                  layer == depth - 1, layer)
        x = x2.reshape(bsz, t, d)
    return x
```

```python
import functools
import math

import jax
import jax.numpy as jnp
from jax import lax
from jax.experimental import pallas as pl
from jax.experimental.pallas import tpu as pltpu

F32 = jnp.float32
BF16 = jnp.bfloat16

D_MODEL = 2048
HEAD_DIM = 64
NSA_HEADS = 16
NSA_KV_HEADS = 4
NSA_GROUP = 4
SWA_HEADS = 16
SWA_KV_HEADS = 2
SWA_GROUP = 8
CMP_LEN = 32
CMP_STRIDE = 16
CMP_HIDDEN = 256
SEL_BLOCK = 64
SEL_SHIFT = 6
N_SEL = 16
NSA_WINDOW = 512
SWA_WINDOW = 128
NUM_BUCKETS = 32
REL_MAX_DISTANCE = 1024
RMS_EPS = 1e-5
NEG_INF = -1e30
FORCE_SCORE = 1e6

QB = 128
NQB = 256
KT = 128
SEL_TILE = 4 * KT
Q_PER_K = NQB // KT
C_PER_Q = NQB // CMP_STRIDE
N_NEAR = 8
N_WIN_TILES = NSA_WINDOW // KT + 1
MASKED = NUM_BUCKETS
ROW_TILE = 512
FFN_ROW_TILE = 1024
FF_TILE = 256
VMEM_LIMIT = 56 * 1024 * 1024

COL_QA = (0, 1024)
COL_KC, COL_VC = (1024, 1280), (1280, 1536)
COL_KS, COL_VS = (1536, 1792), (1792, 2048)
COL_KW, COL_VW = (2048, 2304), (2304, 2560)
COL_GT = (2560, 2608)
COL_QB = (2608, 3632)
COL_KB, COL_VB = (3632, 3760), (3760, 3888)
GATE_ROWS = 16
N_GATES = NSA_KV_HEADS * GATE_ROWS
ROWMAJOR = (COL_KC, COL_VC, COL_KS, COL_KW, COL_KB)


def _dot(a, b):
    return jnp.dot(a, b, preferred_element_type=F32)


def _dot_nt(a, b):
    return lax.dot_general(a, b, (((1,), (1,)), ((), ())), preferred_element_type=F32)


def _rms(x, g):
    return x * lax.rsqrt(jnp.mean(x * x, axis=-1, keepdims=True) + RMS_EPS) * g


def _inproj_kernel(x_ref, g_ref, wt_ref, br_ref, bc_ref, wg_ref, bg_ref,
                   qat_ref, kc_ref, vc_ref, ks_ref, kw_ref, vst_ref, vwt_ref, gtt_ref,
                   qbt_ref, kb_ref, vbt_ref):
    h = _rms(x_ref[0], g_ref[...]).astype(BF16)

    def colmajor(cols):
        a, b = cols
        return _dot_nt(wt_ref[a:b, :], h) + bc_ref[a:b, :]

    off = 0
    for (a, b), ref in zip(ROWMAJOR, (kc_ref, vc_ref, ks_ref, kw_ref, kb_ref)):
        r = _dot_nt(h, wt_ref[a:b, :]) + br_ref[:, off:off + b - a]
        off += b - a
        for hh in range((b - a) // HEAD_DIM):
            ref[0, hh] = r[:, HEAD_DIM * hh:HEAD_DIM * (hh + 1)].astype(ref.dtype)
    qat_ref[0] = (colmajor(COL_QA) * (HEAD_DIM ** -0.5)).astype(qat_ref.dtype)
    qbt_ref[0] = (colmajor(COL_QB) * (HEAD_DIM ** -0.5)).astype(qbt_ref.dtype)
    vst_ref[0] = colmajor(COL_VS).astype(vst_ref.dtype)
    vwt_ref[0] = colmajor(COL_VW).astype(vwt_ref.dtype)
    vbt_ref[0] = colmajor(COL_VB).astype(vbt_ref.dtype)
    gtt_ref[0] = _dot_nt(wg_ref[...], h) + bg_ref[...]


def _inproj(x, g, wt, b_row, b_col, wg, bg, layer):
    bsz, t, d = x.shape
    tm = min(ROW_TILE, t)
    kv4 = lambda dt: jax.ShapeDtypeStruct((bsz, NSA_KV_HEADS, t, HEAD_DIM), dt)
    kv2 = jax.ShapeDtypeStruct((bsz, SWA_KV_HEADS, t, HEAD_DIM), BF16)
    colm = lambda rows, dt: jax.ShapeDtypeStruct((bsz, rows, t), dt)
    out_shape = (colm(1024, BF16), kv4(F32), kv4(F32), kv4(BF16), kv4(BF16),
                 colm(256, BF16), colm(256, BF16), colm(N_GATES, F32),
                 colm(1024, BF16), kv2, colm(128, BF16))
    kv4_spec = pl.BlockSpec((1, NSA_KV_HEADS, tm, HEAD_DIM), lambda bb, i: (bb, 0, i, 0))
    kv2_spec = pl.BlockSpec((1, SWA_KV_HEADS, tm, HEAD_DIM), lambda bb, i: (bb, 0, i, 0))
    colm_spec = lambda rows: pl.BlockSpec((1, rows, tm), lambda bb, i: (bb, 0, i))
    const = lambda a: pl.BlockSpec(a.shape, lambda bb, i: (0, 0), pipeline_mode=pl.Buffered(1))
    return pl.pallas_call(
        _inproj_kernel,
        grid=(bsz, t // tm),
        in_specs=[pl.BlockSpec((1, tm, d), lambda bb, i: (bb, i, 0)),
                  const(g),
                  pl.BlockSpec((None,) + wt.shape[1:], lambda bb, i: (layer, 0, 0), pipeline_mode=pl.Buffered(1)),
                  const(b_row), const(b_col), const(wg), const(bg)],
        out_specs=(colm_spec(1024), kv4_spec, kv4_spec, kv4_spec, kv4_spec,
                   colm_spec(256), colm_spec(256), colm_spec(N_GATES),
                   colm_spec(1024), kv2_spec, colm_spec(128)),
        out_shape=out_shape,
        compiler_params=pltpu.CompilerParams(
            dimension_semantics=("parallel", "parallel"), vmem_limit_bytes=VMEM_LIMIT),
        name="inproj",
    )(x, g, wt, b_row, b_col, wg, bg)


def _compress_kernel(kr_ref, vr_ref, pk_ref, pv_ref, k1_ref, k2_ref, v1_ref, v2t_ref,
                     ko_ref, vot_ref):
    n_chunk = kr_ref.shape[2] // CMP_STRIDE
    half = CMP_STRIDE * HEAD_DIM

    def hidden(r_ref, p_ref, w1_ref):
        top = jnp.zeros((n_chunk, CMP_HIDDEN), F32)
        bot = jnp.zeros((n_chunk, CMP_HIDDEN), F32)
        for l in range(CMP_STRIDE):
            xl = r_ref[0, 0, pl.ds(l, n_chunk, stride=CMP_STRIDE), :]
            a = (xl + p_ref[l:l + 1, :]).astype(BF16)
            top = top + _dot(a, w1_ref[HEAD_DIM * l:HEAD_DIM * (l + 1), :])
            c = (xl + p_ref[CMP_STRIDE + l:CMP_STRIDE + l + 1, :]).astype(BF16)
            bot = bot + _dot(c, w1_ref[half + HEAD_DIM * l:half + HEAD_DIM * (l + 1), :])
        hid = top + pltpu.roll(bot, n_chunk - 1, axis=0)
        return jax.nn.gelu(hid).astype(BF16)

    out = _dot(hidden(kr_ref, pk_ref, k1_ref), k2_ref[...])
    row = lax.broadcasted_iota(jnp.int32, out.shape, 0)
    ko_ref[0, 0] = jnp.where(row < n_chunk - 1, out, 0.0).astype(ko_ref.dtype)
    out_t = _dot_nt(v2t_ref[...], hidden(vr_ref, pv_ref, v1_ref))
    col = lax.broadcasted_iota(jnp.int32, out_t.shape, 1)
    vot_ref[0, 0] = jnp.where(col < n_chunk - 1, out_t, 0.0).astype(vot_ref.dtype)


def _compress(kc_raw, vc_raw, pos_k, pos_v, k1, k2, v1, v2t):
    bsz, hkv, t, dh = kc_raw.shape
    n_chunk = t // CMP_STRIDE
    raw_spec = pl.BlockSpec((1, 1, t, dh), lambda bb, hh: (bb, hh, 0, 0))
    full = lambda a: pl.BlockSpec(a.shape, lambda bb, hh: (0,) * a.ndim)
    return pl.pallas_call(
        _compress_kernel,
        grid=(bsz, hkv),
        in_specs=[raw_spec, raw_spec, full(pos_k), full(pos_v), full(k1), full(k2), full(v1), full(v2t)],
        out_specs=(pl.BlockSpec((1, 1, n_chunk, dh), lambda bb, hh: (bb, hh, 0, 0)),
                   pl.BlockSpec((1, 1, dh, n_chunk), lambda bb, hh: (bb, hh, 0, 0))),
        out_shape=(jax.ShapeDtypeStruct((bsz, hkv, n_chunk, dh), BF16),
                   jax.ShapeDtypeStruct((bsz, hkv, dh, n_chunk), BF16)),
        compiler_params=pltpu.CompilerParams(
            dimension_semantics=("parallel", "parallel"), vmem_limit_bytes=VMEM_LIMIT),
        name="compress",
    )(kc_raw, vc_raw, pos_k, pos_v, k1, k2, v1, v2t)


def _expand_kernel(tbl_ref, idx_ref, o_ref, *, gsz):
    grp = pl.program_id(0)
    idx = idx_ref[0]
    w = idx.shape[1]
    accs = [jnp.full(idx.shape, tbl_ref[MASKED, grp * gsz + g], F32) for g in range(gsz)]
    for b in range(NUM_BUCKETS):
        hit = idx == b
        for g in range(gsz):
            accs[g] = jnp.where(hit, tbl_ref[b, grp * gsz + g], accs[g])
    for g in range(gsz):
        o_ref[0, 0, :, w * g:w * (g + 1)] = accs[g]


def _expand_tables(tbl, idx, gsz):
    r0, r1, w = idx.shape
    heads = tbl.shape[1]
    tr = min(r1, 128)
    tbl_ext = jnp.concatenate([tbl.astype(F32), jnp.full((1, heads), NEG_INF, F32)], axis=0)
    return pl.pallas_call(
        functools.partial(_expand_kernel, gsz=gsz),
        grid=(heads // gsz, r0, r1 // tr),
        in_specs=[pl.BlockSpec(memory_space=pltpu.SMEM),
                  pl.BlockSpec((1, tr, w), lambda gg, a, r: (a, r, 0))],
        out_specs=pl.BlockSpec((1, 1, tr, gsz * w), lambda gg, a, r: (gg, a, r, 0)),
        out_shape=jax.ShapeDtypeStruct((heads // gsz, r0, r1, gsz * w), F32),
        compiler_params=pltpu.CompilerParams(
            dimension_semantics=("parallel", "parallel", "parallel"), vmem_limit_bytes=VMEM_LIMIT),
        name="expand_bias",
    )(tbl_ext, idx)


def _t5_bucket(dist):
    dist = jnp.maximum(dist, 0)
    max_exact = NUM_BUCKETS // 2
    d = jnp.maximum(dist, 1).astype(F32)
    ratio = jnp.log(d / max_exact) / math.log(REL_MAX_DISTANCE / max_exact)
    large = max_exact + (ratio * (NUM_BUCKETS - max_exact)).astype(jnp.int32)
    large = jnp.minimum(large, NUM_BUCKETS - 1)
    return jnp.where(dist < max_exact, dist, large)


def _bias_indices(n_cmp):
    kj = jnp.arange(KT)[:, None]
    qq = jnp.arange(NQB)[None, :]
    d = jnp.arange(-Q_PER_K, N_NEAR + 1)[:, None, None]
    dist = KT * d + qq - kj
    sel_idx = jnp.where(dist >= 0, _t5_bucket(dist), MASKED)
    d = jnp.arange(-(Q_PER_K - 1), N_WIN_TILES + 1)[:, None, None]
    dist = KT * d + qq - kj
    win_idx = jnp.where((dist >= 0) & (dist < NSA_WINDOW), _t5_bucket(dist), MASKED)
    cprime = jnp.arange(2 * n_cmp)[:, None] - (n_cmp - C_PER_Q)
    dist = qq - CMP_STRIDE * cprime - (CMP_LEN - 1)
    cmp_idx = jnp.where(dist >= 0, _t5_bucket(dist), MASKED)[None]
    kr = jnp.arange(2 * KT)[:, None]
    db = jnp.arange(QB)[None, :] + KT - kr
    ok = (db >= 0) & (db < SWA_WINDOW)
    ok = jnp.stack([ok, ok & (kr >= KT)])
    swa_idx = jnp.where(ok, _t5_bucket(db)[None], MASKED)
    i32 = lambda a: a.astype(jnp.int32)
    return i32(sel_idx), i32(win_idx), i32(cmp_idx), i32(swa_idx)


def _softmax_pv(s_tiles, vt_tiles, ones):
    m = s_tiles[0].max(axis=0, keepdims=True)
    for s in s_tiles[1:]:
        m = jnp.maximum(m, s.max(axis=0, keepdims=True))
    acc = None
    for s, vt in zip(s_tiles, vt_tiles):
        p = jnp.exp(s - m).astype(BF16)
        vx = jnp.concatenate([vt, ones[:, :vt.shape[1]]], axis=0)
        part = _dot(vx, p)
        acc = part if acc is None else acc + part
    return acc[:HEAD_DIM] * (1.0 / acc[HEAD_DIM:HEAD_DIM + 1])


def _nsa_kernel(qt_ref, gt_ref, kc_ref, vct_ref, ks_ref, kw_ref, vst_ref, vwt_ref, et_ref,
                bc_ref, sb_ref, wb_ref, ov_ref, o_ref,
                kext_scr, rhs_scr, s_scr, mt_scr, p_scr, al_scr, m_scr, acc_scr):
    qi = pl.program_id(2)
    t0 = qi * NQB
    n_cmp = kc_ref.shape[2]
    n_blk = ks_ref.shape[2] // SEL_BLOCK
    grp = NSA_GROUP
    lanes = grp * NQB

    @pl.when(qi == 0)
    def _():
        kext_scr[:, 0:128] = et_ref[...]
        kext_scr[:, 128:128 + HEAD_DIM] = ks_ref[0, 0]
        kext_scr[:, 128 + HEAD_DIM:] = jnp.zeros((kext_scr.shape[0], 128 - HEAD_DIM), BF16)

    qt = qt_ref[0]
    qst = jnp.concatenate([qt[HEAD_DIM * g:HEAD_DIM * (g + 1)] for g in range(grp)], axis=1)
    ones = jnp.ones((16, SEL_TILE), BF16)

    start = pl.multiple_of(n_cmp - C_PER_Q - C_PER_Q * qi, 8)
    s = _dot(kc_ref[0, 0], qst) + bc_ref[0, 0, pl.ds(start, n_cmp), :]
    e = jnp.exp(s - s.max(axis=0, keepdims=True))
    tq = t0 + (lax.broadcasted_iota(jnp.int32, (1, lanes), 1) & (NQB - 1))
    inv = jnp.where(tq >= CMP_LEN - 1, 1.0 / e.sum(axis=0, keepdims=True), 0.0)
    p = e * inv
    o_c = _dot(vct_ref[0, 0], p.astype(BF16))

    s_tiles, vt_tiles = [], []
    n_win_entries = Q_PER_K - 1 + N_WIN_TILES
    for d in range(-(Q_PER_K - 1), N_WIN_TILES):
        jj = Q_PER_K * qi - d
        k0 = pl.multiple_of(jnp.maximum(jj, 0) * KT, KT)
        entry = jnp.where(jj >= 0, d + Q_PER_K - 1, n_win_entries)
        s_tiles.append(_dot(kw_ref[0, 0, pl.ds(k0, KT), :], qst) + wb_ref[0, entry])
        vt_tiles.append(vwt_ref[0, :, pl.ds(k0, KT)])
    o_w = _softmax_pv(s_tiles, vt_tiles, ones)

    p_sum = p[:, 0:NQB]
    for g in range(1, grp):
        p_sum = p_sum + p[:, NQB * g:NQB * (g + 1)]
    p_hi = p_sum.astype(BF16)
    p_lo = (p_sum - p_hi.astype(F32)).astype(BF16)
    imp = (_dot(ov_ref[...], p_hi) + _dot(ov_ref[...], p_lo))[:n_blk]
    jb = lax.broadcasted_iota(jnp.int32, (n_blk, NQB), 0)
    blk_t = (t0 + lax.broadcasted_iota(jnp.int32, (n_blk, NQB), 1)) >> SEL_SHIFT
    forced = (jb == 0) | (jb == blk_t) | (jb == blk_t - 1)
    val = jnp.where(forced, imp + FORCE_SCORE, imp)
    val = jnp.where(jb > blk_t, NEG_INF, val)
    n_sel = min(N_SEL, n_blk)
    row8 = lax.broadcasted_iota(jnp.int32, (8, NQB), 0)

    def drop_mask(nb):
        def fn(v):
            if nb <= n_sel:
                return jnp.zeros((n_blk, NQB), F32)
            vals = [v[8 * k:8 * (k + 1)] for k in range(nb // 8)]
            ranks = [jnp.zeros((8, NQB), jnp.int32) for _ in vals]
            for j in range(nb):
                r = jnp.broadcast_to(v[j:j + 1], (8, NQB))
                for k in range(nb // 8):
                    if 8 * k > j:
                        beats = r >= vals[k]
                    elif 8 * k + 7 <= j:
                        beats = r > vals[k]
                    else:
                        beats = (r > vals[k]) | ((r == vals[k]) & (row8 + 8 * k > j))
                    ranks[k] = ranks[k] + jnp.where(beats, 1, 0)
            out = [jnp.where(rk < n_sel, 0.0, NEG_INF) for rk in ranks]
            return jnp.concatenate(out + [jnp.zeros((n_blk - nb, NQB), F32)] * (nb < n_blk), axis=0)
        return fn

    visible = (NQB // SEL_BLOCK) * (qi + 1)
    prefixes = list(range(n_sel, n_blk, n_sel)) + [n_blk]
    choose = drop_mask(prefixes[-1])
    for nb in reversed(prefixes[:-1]):
        choose = functools.partial(
            lambda v, nb, rest: lax.cond(visible <= nb, drop_mask(nb), rest, v), nb=nb, rest=choose)
    drop = choose(val)
    drop = jnp.concatenate([drop] + [jnp.zeros((128 - n_blk, NQB), F32)] * (n_blk < 128), axis=0).astype(BF16)
    rhs_scr[...] = jnp.concatenate([jnp.concatenate([drop] * grp, axis=1), qst,
                                    jnp.zeros((128 - HEAD_DIM, lanes), BF16)], axis=0)

    tk = SEL_TILE
    last_tile = kext_scr.shape[0] // tk - 1

    def logits(i):
        k0 = pl.multiple_of(jnp.minimum(i, last_tile) * tk, tk)
        s = _dot(kext_scr[pl.ds(k0, tk), :], rhs_scr[...])
        d = Q_PER_K * qi - (tk // KT) * i
        entries = [jnp.clip(d - c, -Q_PER_K, N_NEAR) + Q_PER_K for c in range(tk // KT)]
        s = s + jnp.concatenate([sb_ref[0, e] for e in entries], axis=0)
        return s, s.max(axis=0, keepdims=True)

    def softmax(s, mt, m_prev):
        m_new = jnp.maximum(m_prev, mt)
        return jnp.exp(s - m_new).astype(BF16), jnp.exp(m_prev - m_new), m_new

    def pv(i, p, alpha, acc):
        k0 = pl.multiple_of(jnp.clip(i, 0, last_tile) * tk, tk)
        vx = jnp.concatenate([vst_ref[0, :, pl.ds(k0, tk)], ones], axis=0)
        return acc * alpha + _dot(vx, p)

    s0, mt0 = logits(0)
    s1, mt1 = logits(1)
    p0, a0, m0 = softmax(s0, mt0, jnp.full((1, lanes), NEG_INF, F32))
    s_scr[1], mt_scr[1] = s1, mt1
    p_scr[0], al_scr[0] = p0, a0
    m_scr[...] = m0
    acc_scr[...] = jnp.zeros(acc_scr.shape, F32)

    def sel_body(i, carry):
        par = i & 1
        p_old, a_old, acc = p_scr[par], al_scr[par], acc_scr[...]
        s_cur, mt_cur, m_prev = s_scr[1 - par], mt_scr[1 - par], m_scr[...]
        s_new, mt_new = logits(i)
        acc_scr[...] = pv(i - 2, p_old, a_old, acc)
        p_cur, a_cur, m_new = softmax(s_cur, mt_cur, m_prev)
        p_scr[1 - par], al_scr[1 - par] = p_cur, a_cur
        m_scr[...] = m_new
        s_scr[par], mt_scr[par] = s_new, mt_new
        return carry

    n_tiles = jnp.maximum((KT * Q_PER_K * (qi + 1) + tk - 1) // tk, 2)
    lax.fori_loop(2, n_tiles, sel_body, 0)
    par = n_tiles & 1
    acc = pv(n_tiles - 2, p_scr[par], al_scr[par], acc_scr[...])
    p_last, a_last, _ = softmax(s_scr[1 - par], mt_scr[1 - par], m_scr[...])
    acc = pv(n_tiles - 1, p_last, a_last, acc)
    o_s = acc[:HEAD_DIM] * (1.0 / acc[HEAD_DIM:HEAD_DIM + 1])

    gate = jax.nn.sigmoid(gt_ref[0])
    outs = []
    for g in range(grp):
        sl = slice(NQB * g, NQB * (g + 1))
        outs.append(gate[3 * g:3 * g + 1] * o_c[:, sl] + gate[3 * g + 1:3 * g + 2] * o_s[:, sl]
                    + gate[3 * g + 2:3 * g + 3] * o_w[:, sl])
    o_ref[0] = jnp.concatenate(outs, axis=0).T.astype(o_ref.dtype)


def _nsa_attention(qat, gtt, kc, vct, ks, kw, vst, vwt, et, bct, sbt, wbt, ovt):
    bsz, _, t = qat.shape
    hkv = NSA_KV_HEADS
    n_cmp = kc.shape[2]
    gw = NSA_GROUP * HEAD_DIM
    lanes = NSA_GROUP * NQB
    kv_spec = pl.BlockSpec((1, 1, t, HEAD_DIM), lambda bb, hh, i: (bb, hh, 0, 0))
    vt_spec = pl.BlockSpec((1, HEAD_DIM, t), lambda bb, hh, i: (bb, hh, 0))
    tbl = lambda a: pl.BlockSpec((1,) + a.shape[1:], lambda bb, hh, i: (hh, 0, 0, 0))
    const = lambda a: pl.BlockSpec(a.shape, lambda bb, hh, i: (0, 0))
    return pl.pallas_call(
        _nsa_kernel,
        grid=(bsz, hkv, t // NQB),
        in_specs=[
            pl.BlockSpec((1, gw, NQB), lambda bb, hh, i: (bb, hh, i)),
            pl.BlockSpec((1, GATE_ROWS, NQB), lambda bb, hh, i: (bb, hh, i)),
            pl.BlockSpec((1, 1, n_cmp, HEAD_DIM), lambda bb, hh, i: (bb, hh, 0, 0)),
            pl.BlockSpec((1, 1, HEAD_DIM, n_cmp), lambda bb, hh, i: (bb, hh, 0, 0)),
            kv_spec, kv_spec, vt_spec, vt_spec, const(et),
            tbl(bct), tbl(sbt), tbl(wbt), const(ovt),
        ],
        out_specs=pl.BlockSpec((1, NQB, gw), lambda bb, hh, i: (bb, i, hh)),
        out_shape=jax.ShapeDtypeStruct((bsz, t, NSA_HEADS * HEAD_DIM), BF16),
        scratch_shapes=[pltpu.VMEM((t, 256), BF16),
                        pltpu.VMEM((256, lanes), BF16),
                        pltpu.VMEM((2, SEL_TILE, lanes), F32),
                        pltpu.VMEM((2, 1, lanes), F32),
                        pltpu.VMEM((2, SEL_TILE, lanes), BF16),
                        pltpu.VMEM((2, 1, lanes), F32),
                        pltpu.VMEM((1, lanes), F32),
                        pltpu.VMEM((HEAD_DIM + 16, lanes), F32)],
        compiler_params=pltpu.CompilerParams(
            dimension_semantics=("parallel", "parallel", "arbitrary"), vmem_limit_bytes=VMEM_LIMIT),
        name="nsa_attention",
    )(qat, gtt, kc, vct, ks, kw, vst, vwt, et, bct, sbt, wbt, ovt)


def _swa_kernel(sink_ref, qt_ref, kp_ref, kn_ref, vp_ref, vn_ref, wb_ref, o_ref):
    qi = pl.program_id(2)
    qt = qt_ref[0]
    qst = jnp.concatenate([qt[HEAD_DIM * g:HEAD_DIM * (g + 1)] for g in range(SWA_GROUP)], axis=1)
    k = jnp.concatenate([kp_ref[0, 0], kn_ref[0, 0]], axis=0)
    vt = jnp.concatenate([vp_ref[0], vn_ref[0]], axis=1)
    s = _dot(k, qst) + wb_ref[0, jnp.where(qi == 0, 1, 0)]
    sink = sink_ref[0]
    m = jnp.maximum(s.max(axis=0, keepdims=True), sink)
    p = jnp.exp(s - m).astype(BF16)
    vx = jnp.concatenate([vt, jnp.ones((16, 2 * KT), BF16)], axis=0)
    acc = _dot(vx, p)
    o = acc[:HEAD_DIM] * (1.0 / (acc[HEAD_DIM:HEAD_DIM + 1] + jnp.exp(sink - m)))
    o = jnp.concatenate([o[:, QB * g:QB * (g + 1)] for g in range(SWA_GROUP)], axis=0)
    o_ref[0] = o.T.astype(o_ref.dtype)


def _swa_attention(sink_rows, qbt, kb, vbt, wbb):
    bsz, _, t = qbt.shape
    gw = SWA_GROUP * HEAD_DIM
    lanes = SWA_GROUP * QB
    kprev = pl.BlockSpec((1, 1, KT, HEAD_DIM), lambda bb, hh, i: (bb, hh, jnp.maximum(i - 1, 0), 0))
    kcur = pl.BlockSpec((1, 1, KT, HEAD_DIM), lambda bb, hh, i: (bb, hh, i, 0))
    vprev = pl.BlockSpec((1, HEAD_DIM, KT), lambda bb, hh, i: (bb, hh, jnp.maximum(i - 1, 0)))
    vcur = pl.BlockSpec((1, HEAD_DIM, KT), lambda bb, hh, i: (bb, hh, i))
    return pl.pallas_call(
        _swa_kernel,
        grid=(bsz, SWA_KV_HEADS, t // QB),
        in_specs=[
            pl.BlockSpec((1, 1, lanes), lambda bb, hh, i: (hh, 0, 0)),
            pl.BlockSpec((1, gw, QB), lambda bb, hh, i: (bb, hh, i)),
            kprev, kcur, vprev, vcur,
            pl.BlockSpec((1,) + wbb.shape[1:], lambda bb, hh, i: (hh, 0, 0, 0)),
        ],
        out_specs=pl.BlockSpec((1, QB, gw), lambda bb, hh, i: (bb, i, hh)),
        out_shape=jax.ShapeDtypeStruct((bsz, t, SWA_HEADS * HEAD_DIM), BF16),
        compiler_params=pltpu.CompilerParams(
            dimension_semantics=("parallel", "parallel", "arbitrary"), vmem_limit_bytes=VMEM_LIMIT),
        name="swa_attention",
    )(sink_rows, qbt, kb, kb, vbt, vbt, wbb)


def _outproj_kernel(x_ref, oa_ref, ob_ref, w_ref, o_ref):
    na = oa_ref.shape[1]
    o_ref[...] = x_ref[...] + _dot(oa_ref[...], w_ref[:na, :]) + _dot(ob_ref[...], w_ref[na:, :])


def _outproj(x, oa, ob, w, layer):
    n, d = x.shape
    tm = min(ROW_TILE, n)
    row = lambda width: pl.BlockSpec((tm, width), lambda i: (i, 0))
    return pl.pallas_call(
        _outproj_kernel,
        grid=(n // tm,),
        in_specs=[row(d), row(oa.shape[1]), row(ob.shape[1]),
                  pl.BlockSpec((None,) + w.shape[1:], lambda i: (layer, 0, 0))],
        out_specs=row(d),
        out_shape=jax.ShapeDtypeStruct((n, d), F32),
        compiler_params=pltpu.CompilerParams(
            dimension_semantics=("parallel",), vmem_limit_bytes=VMEM_LIMIT),
        name="outproj",
    )(x, oa, ob, w)


def _ffn_kernel(x_ref, g_ref, wg_ref, wu_ref, wd_ref, gf_ref, o_ref, h_scr, *, final_norm):
    f = pl.program_id(1)

    @pl.when(f == 0)
    def _():
        x = x_ref[...]
        h_scr[...] = _rms(x, g_ref[...]).astype(BF16)
        o_ref[...] = x

    h = h_scr[...]
    gate = _dot(h, wg_ref[...].astype(BF16))
    act = (gate * jax.nn.sigmoid(gate) * _dot(h, wu_ref[...].astype(BF16))).astype(BF16)
    half = o_ref.shape[1] // 2
    o_ref[:, :half] += _dot(act, wd_ref[:, :half].astype(BF16))
    o_ref[:, half:] += _dot(act, wd_ref[:, half:].astype(BF16))

    if final_norm:
        @pl.when(f == pl.num_programs(1) - 1)
        def _():
            o_ref[...] = _rms(o_ref[...], gf_ref[...])


def _ffn(x, g, wg, wu, wd, g_final, final_norm, layer):
    n, d = x.shape
    dff = wg.shape[2]
    tm = min(FFN_ROW_TILE, n)
    return pl.pallas_call(
        functools.partial(_ffn_kernel, final_norm=final_norm),
        grid=(n // tm, dff // FF_TILE),
        in_specs=[
            pl.BlockSpec((tm, d), lambda i, f: (i, 0)),
            pl.BlockSpec((1, d), lambda i, f: (0, 0)),
            pl.BlockSpec((None, d, FF_TILE), lambda i, f: (layer, 0, f)),
            pl.BlockSpec((None, d, FF_TILE), lambda i, f: (layer, 0, f)),
            pl.BlockSpec((None, FF_TILE, d), lambda i, f: (layer, f, 0)),
            pl.BlockSpec((1, d), lambda i, f: (0, 0)),
        ],
        out_specs=pl.BlockSpec((tm, d), lambda i, f: (i, 0)),
        out_shape=jax.ShapeDtypeStruct((n, d), F32),
        scratch_shapes=[pltpu.VMEM((tm, d), BF16)],
        compiler_params=pltpu.CompilerParams(
            dimension_semantics=("parallel", "arbitrary"), vmem_limit_bytes=VMEM_LIMIT),
        name="ffn",
    )(x, g, wg, wu, wd, g_final)


def _gate_rows(a):
    depth, _, n = a.shape
    a = a.reshape(depth, NSA_KV_HEADS, 3 * NSA_GROUP, n)
    a = jnp.pad(a, ((0, 0), (0, 0), (0, GATE_ROWS - 3 * NSA_GROUP), (0, 0)))
    return a.reshape(depth, N_GATES, n)


def _overlap_t(n_cmp, n_blk):
    ci = jnp.arange(n_cmp)[None, :] * CMP_STRIDE
    sj = jnp.arange(128)[:, None] * SEL_BLOCK
    ov = (ci < sj + SEL_BLOCK) & (ci + CMP_LEN > sj) & (jnp.arange(128)[:, None] < n_blk)
    return (ov & (jnp.arange(n_cmp)[None, :] < n_cmp - 1)).astype(BF16)


def kernel(x, rel_bias, norm_mix, norm_ffn, w_in, b_in, cmp_pos_k, cmp_pos_v, cmp_k_w1, cmp_k_w2,
           cmp_v_w1, cmp_v_w2, sinks, w_out, w_gate, w_up, w_down, norm_final):
    bsz, t, d = x.shape
    depth = w_in.shape[0]
    n_cmp = t // CMP_STRIDE
    sel_idx, win_idx, cmp_idx, swa_idx = _bias_indices(n_cmp)
    tbl_a, tbl_b = rel_bias[:, :NSA_HEADS], rel_bias[:, NSA_HEADS:]
    sbt = _expand_tables(tbl_a, sel_idx, NSA_GROUP)
    wbt = _expand_tables(tbl_a, win_idx, NSA_GROUP)
    bct = _expand_tables(tbl_a, cmp_idx, NSA_GROUP)
    wbb = _expand_tables(tbl_b, swa_idx, SWA_GROUP)
    ovt = _overlap_t(n_cmp, t // SEL_BLOCK)
    et = (jnp.arange(t)[:, None] // SEL_BLOCK == jnp.arange(128)[None, :]).astype(BF16)
    w_in_t = jnp.swapaxes(w_in.astype(BF16), -1, -2)
    b_row = jnp.concatenate([b_in[:, a:b] for a, b in ROWMAJOR], axis=-1).astype(F32)
    w_gt, b_gt = _gate_rows(w_in_t[:, COL_GT[0]:COL_GT[1], :]), _gate_rows(b_in[:, COL_GT[0]:COL_GT[1], None])
    w_out16 = w_out.astype(BF16)
    for layer in range(depth):
        qat, kc_raw, vc_raw, ks, kw, vst, vwt, gtt, qbt, kb, vbt = _inproj(
            x, norm_mix[layer][None, :], w_in_t, b_row[layer][None, :],
            b_in[layer].astype(F32)[:, None], w_gt[layer], b_gt[layer].astype(F32), layer)
        kc, vct = _compress(kc_raw, vc_raw, cmp_pos_k[layer], cmp_pos_v[layer],
                            cmp_k_w1[layer].astype(BF16), cmp_k_w2[layer].astype(BF16),
                            cmp_v_w1[layer].astype(BF16), cmp_v_w2[layer].T.astype(BF16))
        o_a = _nsa_attention(qat, gtt, kc, vct, ks, kw, vst, vwt, et, bct, sbt, wbt, ovt)
        sink_rows = jnp.repeat(sinks[layer].astype(F32), QB).reshape(SWA_KV_HEADS, 1, SWA_GROUP * QB)
        o_b = _swa_attention(sink_rows, qbt, kb, vbt, wbb)
        x2 = _outproj(x.reshape(bsz * t, d), o_a.reshape(bsz * t, -1), o_b.reshape(bsz * t, -1),
                      w_out16, layer)
        x2 = _ffn(x2, norm_ffn[layer][None, :], w_gate, w_up, w_down, norm_final[None, :],
                  layer == depth - 1, layer)
        x = x2.reshape(bsz, t, d)
    return x
```

```python
import functools
import math

import jax
import jax.numpy as jnp
from jax import lax
from jax.experimental import pallas as pl
from jax.experimental.pallas import tpu as pltpu

F32 = jnp.float32
BF16 = jnp.bfloat16

D_MODEL = 2048
HEAD_DIM = 64
NSA_HEADS = 16
NSA_KV_HEADS = 4
NSA_GROUP = 4
SWA_HEADS = 16
SWA_KV_HEADS = 2
SWA_GROUP = 8
CMP_LEN = 32
CMP_STRIDE = 16
CMP_HIDDEN = 256
SEL_BLOCK = 64
SEL_SHIFT = 6
N_SEL = 16
NSA_WINDOW = 512
SWA_WINDOW = 128
NUM_BUCKETS = 32
REL_MAX_DISTANCE = 1024
RMS_EPS = 1e-5
NEG_INF = -1e30
FORCE_SCORE = 1e6

LANES = 128
MXU_DEPTH = 256
QB = 128
NQB = 256
KT = 128
SEL_TILE = 4 * KT
Q_PER_K = NQB // KT
C_PER_Q = NQB // CMP_STRIDE
N_NEAR = 8
N_WIN_TILES = NSA_WINDOW // KT + 1
BLK_LANES = LANES
_MAX_EXACT = NUM_BUCKETS // 2
FAR_DIST = math.ceil(_MAX_EXACT * (REL_MAX_DISTANCE / _MAX_EXACT)
                     ** ((NUM_BUCKETS - _MAX_EXACT - 1) / (NUM_BUCKETS - _MAX_EXACT)))
assert KT * N_NEAR - (KT - 1) >= FAR_DIST
assert BLK_LANES + HEAD_DIM <= MXU_DEPTH
MASKED = NUM_BUCKETS
ROW_TILE = 512
FFN_ROW_TILE = 1024
FF_TILE = 256
VMEM_LIMIT = 56 * 1024 * 1024

COL_QA = (0, 1024)
COL_KC, COL_VC = (1024, 1280), (1280, 1536)
COL_KS, COL_VS = (1536, 1792), (1792, 2048)
COL_KW, COL_VW = (2048, 2304), (2304, 2560)
COL_GT = (2560, 2608)
COL_QB = (2608, 3632)
COL_KB, COL_VB = (3632, 3760), (3760, 3888)
GATE_ROWS = 16
N_GATES = NSA_KV_HEADS * GATE_ROWS
ROWMAJOR = (COL_KC, COL_VC, COL_KS, COL_KW, COL_KB)


def _dot(a, b):
    return jnp.dot(a, b, preferred_element_type=F32)


def _dot_nt(a, b):
    return lax.dot_general(a, b, (((1,), (1,)), ((), ())), preferred_element_type=F32)


def _rms(x, g):
    return x * lax.rsqrt(jnp.mean(x * x, axis=-1, keepdims=True) + RMS_EPS) * g


def _inproj_kernel(x_ref, g_ref, wt_ref, br_ref, bc_ref, wg_ref, bg_ref,
                   qat_ref, kc_ref, vc_ref, ks_ref, kw_ref, vst_ref, vwt_ref, gtt_ref,
                   qbt_ref, kb_ref, vbt_ref):
    h = _rms(x_ref[0], g_ref[...]).astype(BF16)

    def colmajor(cols):
        a, b = cols
        return _dot_nt(wt_ref[a:b, :], h) + bc_ref[a:b, :]

    off = 0
    for (a, b), ref in zip(ROWMAJOR, (kc_ref, vc_ref, ks_ref, kw_ref, kb_ref)):
        r = _dot_nt(h, wt_ref[a:b, :]) + br_ref[:, off:off + b - a]
        off += b - a
        for hh in range((b - a) // HEAD_DIM):
            ref[0, hh] = r[:, HEAD_DIM * hh:HEAD_DIM * (hh + 1)].astype(ref.dtype)
    qat_ref[0] = (colmajor(COL_QA) * (HEAD_DIM ** -0.5)).astype(qat_ref.dtype)
    qbt_ref[0] = (colmajor(COL_QB) * (HEAD_DIM ** -0.5)).astype(qbt_ref.dtype)
    vst_ref[0] = colmajor(COL_VS).astype(vst_ref.dtype)
    vwt_ref[0] = colmajor(COL_VW).astype(vwt_ref.dtype)
    vbt_ref[0] = colmajor(COL_VB).astype(vbt_ref.dtype)
    gtt_ref[0] = _dot_nt(wg_ref[...], h) + bg_ref[...]


def _inproj(x, g, wt, b_row, b_col, wg, bg, layer):
    bsz, t, d = x.shape
    tm = min(ROW_TILE, t)
    kv4 = lambda dt: jax.ShapeDtypeStruct((bsz, NSA_KV_HEADS, t, HEAD_DIM), dt)
    kv2 = jax.ShapeDtypeStruct((bsz, SWA_KV_HEADS, t, HEAD_DIM), BF16)
    colm = lambda rows, dt: jax.ShapeDtypeStruct((bsz, rows, t), dt)
    out_shape = (colm(1024, BF16), kv4(F32), kv4(F32), kv4(BF16), kv4(BF16),
                 colm(256, BF16), colm(256, BF16), colm(N_GATES, F32),
                 colm(1024, BF16), kv2, colm(128, BF16))
    kv4_spec = pl.BlockSpec((1, NSA_KV_HEADS, tm, HEAD_DIM), lambda bb, i: (bb, 0, i, 0))
    kv2_spec = pl.BlockSpec((1, SWA_KV_HEADS, tm, HEAD_DIM), lambda bb, i: (bb, 0, i, 0))
    colm_spec = lambda rows: pl.BlockSpec((1, rows, tm), lambda bb, i: (bb, 0, i))
    const = lambda a: pl.BlockSpec(a.shape, lambda bb, i: (0, 0), pipeline_mode=pl.Buffered(1))
    return pl.pallas_call(
        _inproj_kernel,
        grid=(bsz, t // tm),
        in_specs=[pl.BlockSpec((1, tm, d), lambda bb, i: (bb, i, 0)),
                  const(g),
                  pl.BlockSpec((None,) + wt.shape[1:], lambda bb, i: (layer, 0, 0), pipeline_mode=pl.Buffered(1)),
                  const(b_row), const(b_col), const(wg), const(bg)],
        out_specs=(colm_spec(1024), kv4_spec, kv4_spec, kv4_spec, kv4_spec,
                   colm_spec(256), colm_spec(256), colm_spec(N_GATES),
                   colm_spec(1024), kv2_spec, colm_spec(128)),
        out_shape=out_shape,
        compiler_params=pltpu.CompilerParams(
            dimension_semantics=("parallel", "parallel"), vmem_limit_bytes=VMEM_LIMIT),
        name="inproj",
    )(x, g, wt, b_row, b_col, wg, bg)


def _compress_kernel(kr_ref, vr_ref, pk_ref, pv_ref, k1_ref, k2_ref, v1_ref, v2t_ref,
                     ko_ref, vot_ref):
    n_chunk = kr_ref.shape[2] // CMP_STRIDE
    half = CMP_STRIDE * HEAD_DIM

    def hidden(r_ref, p_ref, w1_ref):
        top = jnp.zeros((n_chunk, CMP_HIDDEN), F32)
        bot = jnp.zeros((n_chunk, CMP_HIDDEN), F32)
        for l in range(CMP_STRIDE):
            xl = r_ref[0, 0, pl.ds(l, n_chunk, stride=CMP_STRIDE), :]
            a = (xl + p_ref[l:l + 1, :]).astype(BF16)
            top = top + _dot(a, w1_ref[HEAD_DIM * l:HEAD_DIM * (l + 1), :])
            c = (xl + p_ref[CMP_STRIDE + l:CMP_STRIDE + l + 1, :]).astype(BF16)
            bot = bot + _dot(c, w1_ref[half + HEAD_DIM * l:half + HEAD_DIM * (l + 1), :])
        hid = top + pltpu.roll(bot, n_chunk - 1, axis=0)
        return jax.nn.gelu(hid).astype(BF16)

    out = _dot(hidden(kr_ref, pk_ref, k1_ref), k2_ref[...])
    row = lax.broadcasted_iota(jnp.int32, out.shape, 0)
    ko_ref[0, 0] = jnp.where(row < n_chunk - 1, out, 0.0).astype(ko_ref.dtype)
    out_t = _dot_nt(v2t_ref[...], hidden(vr_ref, pv_ref, v1_ref))
    col = lax.broadcasted_iota(jnp.int32, out_t.shape, 1)
    vot_ref[0, 0] = jnp.where(col < n_chunk - 1, out_t, 0.0).astype(vot_ref.dtype)


def _compress(kc_raw, vc_raw, pos_k, pos_v, k1, k2, v1, v2t):
    bsz, hkv, t, dh = kc_raw.shape
    n_chunk = t // CMP_STRIDE
    raw_spec = pl.BlockSpec((1, 1, t, dh), lambda bb, hh: (bb, hh, 0, 0))
    full = lambda a: pl.BlockSpec(a.shape, lambda bb, hh: (0,) * a.ndim)
    return pl.pallas_call(
        _compress_kernel,
        grid=(bsz, hkv),
        in_specs=[raw_spec, raw_spec, full(pos_k), full(pos_v), full(k1), full(k2), full(v1), full(v2t)],
        out_specs=(pl.BlockSpec((1, 1, n_chunk, dh), lambda bb, hh: (bb, hh, 0, 0)),
                   pl.BlockSpec((1, 1, dh, n_chunk), lambda bb, hh: (bb, hh, 0, 0))),
        out_shape=(jax.ShapeDtypeStruct((bsz, hkv, n_chunk, dh), BF16),
                   jax.ShapeDtypeStruct((bsz, hkv, dh, n_chunk), BF16)),
        compiler_params=pltpu.CompilerParams(
            dimension_semantics=("parallel", "parallel"), vmem_limit_bytes=VMEM_LIMIT),
        name="compress",
    )(kc_raw, vc_raw, pos_k, pos_v, k1, k2, v1, v2t)


def _expand_kernel(tbl_ref, idx_ref, o_ref, *, gsz):
    grp = pl.program_id(0)
    idx = idx_ref[0]
    w = idx.shape[1]
    accs = [jnp.full(idx.shape, tbl_ref[MASKED, grp * gsz + g], F32) for g in range(gsz)]
    for b in range(NUM_BUCKETS):
        hit = idx == b
        for g in range(gsz):
            accs[g] = jnp.where(hit, tbl_ref[b, grp * gsz + g], accs[g])
    for g in range(gsz):
        o_ref[0, 0, :, w * g:w * (g + 1)] = accs[g]


def _expand_tables(tbl, idx, gsz):
    r0, r1, w = idx.shape
    heads = tbl.shape[1]
    tr = min(r1, KT)
    tbl_ext = jnp.concatenate([tbl.astype(F32), jnp.full((1, heads), NEG_INF, F32)], axis=0)
    return pl.pallas_call(
        functools.partial(_expand_kernel, gsz=gsz),
        grid=(heads // gsz, r0, r1 // tr),
        in_specs=[pl.BlockSpec(memory_space=pltpu.SMEM),
                  pl.BlockSpec((1, tr, w), lambda gg, a, r: (a, r, 0))],
        out_specs=pl.BlockSpec((1, 1, tr, gsz * w), lambda gg, a, r: (gg, a, r, 0)),
        out_shape=jax.ShapeDtypeStruct((heads // gsz, r0, r1, gsz * w), F32),
        compiler_params=pltpu.CompilerParams(
            dimension_semantics=("parallel", "parallel", "parallel"), vmem_limit_bytes=VMEM_LIMIT),
        name="expand_bias",
    )(tbl_ext, idx)


def _t5_bucket(dist):
    dist = jnp.maximum(dist, 0)
    max_exact = NUM_BUCKETS // 2
    d = jnp.maximum(dist, 1).astype(F32)
    ratio = jnp.log(d / max_exact) / math.log(REL_MAX_DISTANCE / max_exact)
    large = max_exact + (ratio * (NUM_BUCKETS - max_exact)).astype(jnp.int32)
    large = jnp.minimum(large, NUM_BUCKETS - 1)
    return jnp.where(dist < max_exact, dist, large)


def _bias_indices(n_cmp):
    kj = jnp.arange(KT)[:, None]
    qq = jnp.arange(NQB)[None, :]
    d = jnp.arange(-Q_PER_K, N_NEAR + 1)[:, None, None]
    dist = KT * d + qq - kj
    sel_idx = jnp.where(dist >= 0, _t5_bucket(dist), MASKED)
    d = jnp.arange(-(Q_PER_K - 1), N_WIN_TILES + 1)[:, None, None]
    dist = KT * d + qq - kj
    win_idx = jnp.where((dist >= 0) & (dist < NSA_WINDOW), _t5_bucket(dist), MASKED)
    cprime = jnp.arange(2 * n_cmp)[:, None] - (n_cmp - C_PER_Q)
    dist = qq - CMP_STRIDE * cprime - (CMP_LEN - 1)
    cmp_idx = jnp.where(dist >= 0, _t5_bucket(dist), MASKED)[None]
    kr = jnp.arange(2 * KT)[:, None]
    db = jnp.arange(QB)[None, :] + KT - kr
    ok = (db >= 0) & (db < SWA_WINDOW)
    ok = jnp.stack([ok, ok & (kr >= KT)])
    swa_idx = jnp.where(ok, _t5_bucket(db)[None], MASKED)
    i32 = lambda a: a.astype(jnp.int32)
    return i32(sel_idx), i32(win_idx), i32(cmp_idx), i32(swa_idx)


def _softmax_pv(s_tiles, vt_tiles, ones):
    m = s_tiles[0].max(axis=0, keepdims=True)
    for s in s_tiles[1:]:
        m = jnp.maximum(m, s.max(axis=0, keepdims=True))
    acc = None
    for s, vt in zip(s_tiles, vt_tiles):
        p = jnp.exp(s - m).astype(BF16)
        vx = jnp.concatenate([vt, ones[:, :vt.shape[1]]], axis=0)
        part = _dot(vx, p)
        acc = part if acc is None else acc + part
    return acc[:HEAD_DIM] * (1.0 / acc[HEAD_DIM:HEAD_DIM + 1])


def _nsa_kernel(qt_ref, gt_ref, kc_ref, vct_ref, ks_ref, kw_ref, vst_ref, vwt_ref, et_ref,
                bc_ref, sb_ref, wb_ref, ov_ref, o_ref,
                kext_scr, rhs_scr, s_scr, mt_scr, p_scr, al_scr, m_scr, acc_scr):
    qi = pl.program_id(2)
    t0 = qi * NQB
    n_cmp = kc_ref.shape[2]
    n_blk = ks_ref.shape[2] // SEL_BLOCK
    grp = NSA_GROUP
    lanes = grp * NQB

    @pl.when(qi == 0)
    def _():
        kext_scr[:, 0:BLK_LANES] = et_ref[...]
        kext_scr[:, BLK_LANES:BLK_LANES + HEAD_DIM] = ks_ref[0, 0]
        kext_scr[:, BLK_LANES + HEAD_DIM:] = jnp.zeros(
            (kext_scr.shape[0], MXU_DEPTH - BLK_LANES - HEAD_DIM), BF16)

    qt = qt_ref[0]
    qst = jnp.concatenate([qt[HEAD_DIM * g:HEAD_DIM * (g + 1)] for g in range(grp)], axis=1)
    ones = jnp.ones((16, SEL_TILE), BF16)

    start = pl.multiple_of(n_cmp - C_PER_Q - C_PER_Q * qi, 8)
    s = _dot(kc_ref[0, 0], qst) + bc_ref[0, 0, pl.ds(start, n_cmp), :]
    e = jnp.exp(s - s.max(axis=0, keepdims=True))
    tq = t0 + (lax.broadcasted_iota(jnp.int32, (1, lanes), 1) & (NQB - 1))
    inv = jnp.where(tq >= CMP_LEN - 1, 1.0 / e.sum(axis=0, keepdims=True), 0.0)
    p = e * inv
    o_c = _dot(vct_ref[0, 0], p.astype(BF16))

    s_tiles, vt_tiles = [], []
    n_win_entries = Q_PER_K - 1 + N_WIN_TILES
    for d in range(-(Q_PER_K - 1), N_WIN_TILES):
        jj = Q_PER_K * qi - d
        k0 = pl.multiple_of(jnp.maximum(jj, 0) * KT, KT)
        entry = jnp.where(jj >= 0, d + Q_PER_K - 1, n_win_entries)
        s_tiles.append(_dot(kw_ref[0, 0, pl.ds(k0, KT), :], qst) + wb_ref[0, entry])
        vt_tiles.append(vwt_ref[0, :, pl.ds(k0, KT)])
    o_w = _softmax_pv(s_tiles, vt_tiles, ones)

    p_sum = p[:, 0:NQB]
    for g in range(1, grp):
        p_sum = p_sum + p[:, NQB * g:NQB * (g + 1)]
    p_hi = p_sum.astype(BF16)
    p_lo = (p_sum - p_hi.astype(F32)).astype(BF16)
    imp = (_dot(ov_ref[...], p_hi) + _dot(ov_ref[...], p_lo))[:n_blk]
    jb = lax.broadcasted_iota(jnp.int32, (n_blk, NQB), 0)
    blk_t = (t0 + lax.broadcasted_iota(jnp.int32, (n_blk, NQB), 1)) >> SEL_SHIFT
    forced = (jb == 0) | (jb == blk_t) | (jb == blk_t - 1)
    val = jnp.where(forced, imp + FORCE_SCORE, imp)
    val = jnp.where(jb > blk_t, NEG_INF, val)
    n_sel = min(N_SEL, n_blk)
    row8 = lax.broadcasted_iota(jnp.int32, (8, NQB), 0)

    def drop_mask(nb):
        def fn(v):
            if nb <= n_sel:
                return jnp.zeros((n_blk, NQB), F32)
            vals = [v[8 * k:8 * (k + 1)] for k in range(nb // 8)]
            ranks = [jnp.zeros((8, NQB), jnp.int32) for _ in vals]
            for j in range(nb):
                r = jnp.broadcast_to(v[j:j + 1], (8, NQB))
                for k in range(nb // 8):
                    if 8 * k > j:
                        beats = r >= vals[k]
                    elif 8 * k + 7 <= j:
                        beats = r > vals[k]
                    else:
                        beats = (r > vals[k]) | ((r == vals[k]) & (row8 + 8 * k > j))
                    ranks[k] = ranks[k] + jnp.where(beats, 1, 0)
            out = [jnp.where(rk < n_sel, 0.0, NEG_INF) for rk in ranks]
            return jnp.concatenate(out + [jnp.zeros((n_blk - nb, NQB), F32)] * (nb < n_blk), axis=0)
        return fn

    visible = (NQB // SEL_BLOCK) * (qi + 1)
    prefixes = list(range(n_sel, n_blk, n_sel)) + [n_blk]
    choose = drop_mask(prefixes[-1])
    for nb in reversed(prefixes[:-1]):
        choose = functools.partial(
            lambda v, nb, rest: lax.cond(visible <= nb, drop_mask(nb), rest, v), nb=nb, rest=choose)
    drop = choose(val)
    drop = jnp.concatenate([drop] + [jnp.zeros((BLK_LANES - n_blk, NQB), F32)] * (n_blk < BLK_LANES),
                           axis=0).astype(BF16)
    rhs_scr[...] = jnp.concatenate([jnp.concatenate([drop] * grp, axis=1), qst,
                                    jnp.zeros((MXU_DEPTH - BLK_LANES - HEAD_DIM, lanes), BF16)], axis=0)

    tk = SEL_TILE
    last_tile = kext_scr.shape[0] // tk - 1

    def logits(i):
        k0 = pl.multiple_of(jnp.minimum(i, last_tile) * tk, tk)
        s = _dot(kext_scr[pl.ds(k0, tk), :], rhs_scr[...])
        d = Q_PER_K * qi - (tk // KT) * i
        entries = [jnp.clip(d - c, -Q_PER_K, N_NEAR) + Q_PER_K for c in range(tk // KT)]
        s = s + jnp.concatenate([sb_ref[0, e] for e in entries], axis=0)
        return s, s.max(axis=0, keepdims=True)

    def softmax(s, mt, m_prev):
        m_new = jnp.maximum(m_prev, mt)
        return jnp.exp(s - m_new).astype(BF16), jnp.exp(m_prev - m_new), m_new

    def pv(i, p, alpha, acc):
        k0 = pl.multiple_of(jnp.clip(i, 0, last_tile) * tk, tk)
        vx = jnp.concatenate([vst_ref[0, :, pl.ds(k0, tk)], ones], axis=0)
        return acc * alpha + _dot(vx, p)

    s0, mt0 = logits(0)
    s1, mt1 = logits(1)
    p0, a0, m0 = softmax(s0, mt0, jnp.full((1, lanes), NEG_INF, F32))
    s_scr[1], mt_scr[1] = s1, mt1
    p_scr[0], al_scr[0] = p0, a0
    m_scr[...] = m0
    acc_scr[...] = jnp.zeros(acc_scr.shape, F32)

    def sel_body(i, carry):
        par = i & 1
        p_old, a_old, acc = p_scr[par], al_scr[par], acc_scr[...]
        s_cur, mt_cur, m_prev = s_scr[1 - par], mt_scr[1 - par], m_scr[...]
        s_new, mt_new = logits(i)
        acc_scr[...] = pv(i - 2, p_old, a_old, acc)
        p_cur, a_cur, m_new = softmax(s_cur, mt_cur, m_prev)
        p_scr[1 - par], al_scr[1 - par] = p_cur, a_cur
        m_scr[...] = m_new
        s_scr[par], mt_scr[par] = s_new, mt_new
        return carry

    n_tiles = jnp.maximum((KT * Q_PER_K * (qi + 1) + tk - 1) // tk, 2)
    lax.fori_loop(2, n_tiles, sel_body, 0)
    par = n_tiles & 1
    acc = pv(n_tiles - 2, p_scr[par], al_scr[par], acc_scr[...])
    p_last, a_last, _ = softmax(s_scr[1 - par], mt_scr[1 - par], m_scr[...])
    acc = pv(n_tiles - 1, p_last, a_last, acc)
    o_s = acc[:HEAD_DIM] * (1.0 / acc[HEAD_DIM:HEAD_DIM + 1])

    gate = jax.nn.sigmoid(gt_ref[0])
    outs = []
    for g in range(grp):
        sl = slice(NQB * g, NQB * (g + 1))
        outs.append(gate[3 * g:3 * g + 1] * o_c[:, sl] + gate[3 * g + 1:3 * g + 2] * o_s[:, sl]
                    + gate[3 * g + 2:3 * g + 3] * o_w[:, sl])
    o_ref[0] = jnp.concatenate(outs, axis=0).T.astype(o_ref.dtype)


def _nsa_attention(qat, gtt, kc, vct, ks, kw, vst, vwt, et, bct, sbt, wbt, ovt):
    bsz, _, t = qat.shape
    hkv = NSA_KV_HEADS
    n_cmp = kc.shape[2]
    gw = NSA_GROUP * HEAD_DIM
    lanes = NSA_GROUP * NQB
    kv_spec = pl.BlockSpec((1, 1, t, HEAD_DIM), lambda hh, bb, i: (bb, hh, 0, 0))
    vt_spec = pl.BlockSpec((1, HEAD_DIM, t), lambda hh, bb, i: (bb, hh, 0))
    tbl = lambda a: pl.BlockSpec((1,) + a.shape[1:], lambda hh, bb, i: (hh, 0, 0, 0))
    const = lambda a: pl.BlockSpec(a.shape, lambda hh, bb, i: (0, 0))
    return pl.pallas_call(
        _nsa_kernel,
        grid=(hkv, bsz, t // NQB),
        in_specs=[
            pl.BlockSpec((1, gw, NQB), lambda hh, bb, i: (bb, hh, i)),
            pl.BlockSpec((1, GATE_ROWS, NQB), lambda hh, bb, i: (bb, hh, i)),
            pl.BlockSpec((1, 1, n_cmp, HEAD_DIM), lambda hh, bb, i: (bb, hh, 0, 0)),
            pl.BlockSpec((1, 1, HEAD_DIM, n_cmp), lambda hh, bb, i: (bb, hh, 0, 0)),
            kv_spec, kv_spec, vt_spec, vt_spec, const(et),
            tbl(bct), tbl(sbt), tbl(wbt), const(ovt),
        ],
        out_specs=pl.BlockSpec((1, NQB, gw), lambda hh, bb, i: (bb, i, hh)),
        out_shape=jax.ShapeDtypeStruct((bsz, t, NSA_HEADS * HEAD_DIM), BF16),
        scratch_shapes=[pltpu.VMEM((t, MXU_DEPTH), BF16),
                        pltpu.VMEM((MXU_DEPTH, lanes), BF16),
                        pltpu.VMEM((2, SEL_TILE, lanes), F32),
                        pltpu.VMEM((2, 1, lanes), F32),
                        pltpu.VMEM((2, SEL_TILE, lanes), BF16),
                        pltpu.VMEM((2, 1, lanes), F32),
                        pltpu.VMEM((1, lanes), F32),
                        pltpu.VMEM((HEAD_DIM + 16, lanes), F32)],
        compiler_params=pltpu.CompilerParams(
            dimension_semantics=("parallel", "parallel", "arbitrary"), vmem_limit_bytes=VMEM_LIMIT),
        name="nsa_attention",
    )(qat, gtt, kc, vct, ks, kw, vst, vwt, et, bct, sbt, wbt, ovt)


def _swa_kernel(sink_ref, qt_ref, kp_ref, kn_ref, vp_ref, vn_ref, wb_ref, o_ref):
    qi = pl.program_id(2)
    qt = qt_ref[0]
    qst = jnp.concatenate([qt[HEAD_DIM * g:HEAD_DIM * (g + 1)] for g in range(SWA_GROUP)], axis=1)
    k = jnp.concatenate([kp_ref[0, 0], kn_ref[0, 0]], axis=0)
    vt = jnp.concatenate([vp_ref[0], vn_ref[0]], axis=1)
    s = _dot(k, qst) + wb_ref[0, jnp.where(qi == 0, 1, 0)]
    sink = sink_ref[0]
    m = jnp.maximum(s.max(axis=0, keepdims=True), sink)
    p = jnp.exp(s - m).astype(BF16)
    vx = jnp.concatenate([vt, jnp.ones((16, 2 * KT), BF16)], axis=0)
    acc = _dot(vx, p)
    o = acc[:HEAD_DIM] * (1.0 / (acc[HEAD_DIM:HEAD_DIM + 1] + jnp.exp(sink - m)))
    o = jnp.concatenate([o[:, QB * g:QB * (g + 1)] for g in range(SWA_GROUP)], axis=0)
    o_ref[0] = o.T.astype(o_ref.dtype)


def _swa_attention(sink_rows, qbt, kb, vbt, wbb):
    bsz, _, t = qbt.shape
    gw = SWA_GROUP * HEAD_DIM
    lanes = SWA_GROUP * QB
    kprev = pl.BlockSpec((1, 1, KT, HEAD_DIM), lambda bb, hh, i: (bb, hh, jnp.maximum(i - 1, 0), 0))
    kcur = pl.BlockSpec((1, 1, KT, HEAD_DIM), lambda bb, hh, i: (bb, hh, i, 0))
    vprev = pl.BlockSpec((1, HEAD_DIM, KT), lambda bb, hh, i: (bb, hh, jnp.maximum(i - 1, 0)))
    vcur = pl.BlockSpec((1, HEAD_DIM, KT), lambda bb, hh, i: (bb, hh, i))
    return pl.pallas_call(
        _swa_kernel,
        grid=(bsz, SWA_KV_HEADS, t // QB),
        in_specs=[
            pl.BlockSpec((1, 1, lanes), lambda bb, hh, i: (hh, 0, 0)),
            pl.BlockSpec((1, gw, QB), lambda bb, hh, i: (bb, hh, i)),
            kprev, kcur, vprev, vcur,
            pl.BlockSpec((1,) + wbb.shape[1:], lambda bb, hh, i: (hh, 0, 0, 0)),
        ],
        out_specs=pl.BlockSpec((1, QB, gw), lambda bb, hh, i: (bb, i, hh)),
        out_shape=jax.ShapeDtypeStruct((bsz, t, SWA_HEADS * HEAD_DIM), BF16),
        compiler_params=pltpu.CompilerParams(
            dimension_semantics=("parallel", "parallel", "arbitrary"), vmem_limit_bytes=VMEM_LIMIT),
        name="swa_attention",
    )(sink_rows, qbt, kb, kb, vbt, vbt, wbb)


def _outproj_kernel(x_ref, oa_ref, ob_ref, w_ref, o_ref):
    na = oa_ref.shape[1]
    o_ref[...] = x_ref[...] + _dot(oa_ref[...], w_ref[:na, :]) + _dot(ob_ref[...], w_ref[na:, :])


def _outproj(x, oa, ob, w, layer):
    n, d = x.shape
    tm = min(ROW_TILE, n)
    row = lambda width: pl.BlockSpec((tm, width), lambda i: (i, 0))
    return pl.pallas_call(
        _outproj_kernel,
        grid=(n // tm,),
        in_specs=[row(d), row(oa.shape[1]), row(ob.shape[1]),
                  pl.BlockSpec((None,) + w.shape[1:], lambda i: (layer, 0, 0))],
        out_specs=row(d),
        out_shape=jax.ShapeDtypeStruct((n, d), F32),
        compiler_params=pltpu.CompilerParams(
            dimension_semantics=("parallel",), vmem_limit_bytes=VMEM_LIMIT),
        name="outproj",
    )(x, oa, ob, w)


def _ffn_kernel(x_ref, g_ref, wg_ref, wu_ref, wd_ref, gf_ref, o_ref, h_scr, *, final_norm):
    f = pl.program_id(1)

    @pl.when(f == 0)
    def _():
        x = x_ref[...]
        h_scr[...] = _rms(x, g_ref[...]).astype(BF16)
        o_ref[...] = x

    h = h_scr[...]
    gate = _dot(h, wg_ref[...].astype(BF16))
    act = (gate * jax.nn.sigmoid(gate) * _dot(h, wu_ref[...].astype(BF16))).astype(BF16)
    half = o_ref.shape[1] // 2
    o_ref[:, :half] += _dot(act, wd_ref[:, :half].astype(BF16))
    o_ref[:, half:] += _dot(act, wd_ref[:, half:].astype(BF16))

    if final_norm:
        @pl.when(f == pl.num_programs(1) - 1)
        def _():
            o_ref[...] = _rms(o_ref[...], gf_ref[...])


def _ffn(x, g, wg, wu, wd, g_final, final_norm, layer):
    n, d = x.shape
    dff = wg.shape[2]
    tm = min(FFN_ROW_TILE, n)
    return pl.pallas_call(
        functools.partial(_ffn_kernel, final_norm=final_norm),
        grid=(n // tm, dff // FF_TILE),
        in_specs=[
            pl.BlockSpec((tm, d), lambda i, f: (i, 0)),
            pl.BlockSpec((1, d), lambda i, f: (0, 0)),
            pl.BlockSpec((None, d, FF_TILE), lambda i, f: (layer, 0, f)),
            pl.BlockSpec((None, d, FF_TILE), lambda i, f: (layer, 0, f)),
            pl.BlockSpec((None, FF_TILE, d), lambda i, f: (layer, f, 0)),
            pl.BlockSpec((1, d), lambda i, f: (0, 0)),
        ],
        out_specs=pl.BlockSpec((tm, d), lambda i, f: (i, 0)),
        out_shape=jax.ShapeDtypeStruct((n, d), F32),
        scratch_shapes=[pltpu.VMEM((tm, d), BF16)],
        compiler_params=pltpu.CompilerParams(
            dimension_semantics=("parallel", "arbitrary"), vmem_limit_bytes=VMEM_LIMIT),
        name="ffn",
    )(x, g, wg, wu, wd, g_final)


def _gate_rows(a):
    depth, _, n = a.shape
    a = a.reshape(depth, NSA_KV_HEADS, 3 * NSA_GROUP, n)
    a = jnp.pad(a, ((0, 0), (0, 0), (0, GATE_ROWS - 3 * NSA_GROUP), (0, 0)))
    return a.reshape(depth, N_GATES, n)


def _overlap_t(n_cmp, n_blk):
    ci = jnp.arange(n_cmp)[None, :] * CMP_STRIDE
    blk = jnp.arange(BLK_LANES)[:, None]
    sj = blk * SEL_BLOCK
    ov = (ci < sj + SEL_BLOCK) & (ci + CMP_LEN > sj) & (blk < n_blk)
    return (ov & (jnp.arange(n_cmp)[None, :] < n_cmp - 1)).astype(BF16)


def kernel(x, rel_bias, norm_mix, norm_ffn, w_in, b_in, cmp_pos_k, cmp_pos_v, cmp_k_w1, cmp_k_w2,
           cmp_v_w1, cmp_v_w2, sinks, w_out, w_gate, w_up, w_down, norm_final):
    bsz, t, d = x.shape
    depth = w_in.shape[0]
    assert d == D_MODEL and w_in.shape[2] == COL_VB[1] and rel_bias.shape == (NUM_BUCKETS, NSA_HEADS + SWA_HEADS)
    assert t % SEL_TILE == 0 and t // SEL_TILE >= 2 and t // SEL_BLOCK <= BLK_LANES
    assert (bsz * t) % FFN_ROW_TILE == 0 and w_gate.shape[2] % FF_TILE == 0
    n_cmp = t // CMP_STRIDE
    sel_idx, win_idx, cmp_idx, swa_idx = _bias_indices(n_cmp)
    tbl_a, tbl_b = rel_bias[:, :NSA_HEADS], rel_bias[:, NSA_HEADS:]
    sbt = _expand_tables(tbl_a, sel_idx, NSA_GROUP)
    wbt = _expand_tables(tbl_a, win_idx, NSA_GROUP)
    bct = _expand_tables(tbl_a, cmp_idx, NSA_GROUP)
    wbb = _expand_tables(tbl_b, swa_idx, SWA_GROUP)
    ovt = _overlap_t(n_cmp, t // SEL_BLOCK)
    et = (jnp.arange(t)[:, None] // SEL_BLOCK == jnp.arange(BLK_LANES)[None, :]).astype(BF16)
    w_in_t = jnp.swapaxes(w_in.astype(BF16), -1, -2)
    b_row = jnp.concatenate([b_in[:, a:b] for a, b in ROWMAJOR], axis=-1).astype(F32)
    w_gt, b_gt = _gate_rows(w_in_t[:, COL_GT[0]:COL_GT[1], :]), _gate_rows(b_in[:, COL_GT[0]:COL_GT[1], None])
    w_out16 = w_out.astype(BF16)
    for layer in range(depth):
        qat, kc_raw, vc_raw, ks, kw, vst, vwt, gtt, qbt, kb, vbt = _inproj(
            x, norm_mix[layer][None, :], w_in_t, b_row[layer][None, :],
            b_in[layer].astype(F32)[:, None], w_gt[layer], b_gt[layer].astype(F32), layer)
        kc, vct = _compress(kc_raw, vc_raw, cmp_pos_k[layer], cmp_pos_v[layer],
                            cmp_k_w1[layer].astype(BF16), cmp_k_w2[layer].astype(BF16),
                            cmp_v_w1[layer].astype(BF16), cmp_v_w2[layer].T.astype(BF16))
        o_a = _nsa_attention(qat, gtt, kc, vct, ks, kw, vst, vwt, et, bct, sbt, wbt, ovt)
        sink_rows = jnp.repeat(sinks[layer].astype(F32), QB).reshape(SWA_KV_HEADS, 1, SWA_GROUP * QB)
        o_b = _swa_attention(sink_rows, qbt, kb, vbt, wbb)
        x2 = _outproj(x.reshape(bsz * t, d), o_a.reshape(bsz * t, -1), o_b.reshape(bsz * t, -1),
                      w_out16, layer)
        x2 = _ffn(x2, norm_ffn[layer][None, :], w_gate, w_up, w_down, norm_final[None, :],
                  layer == depth - 1, layer)
        x = x2.reshape(bsz, t, d)
    return x
```

```python
import functools
import math

import jax
import jax.numpy as jnp
from jax import lax
from jax.experimental import pallas as pl
from jax.experimental.pallas import tpu as pltpu

F32 = jnp.float32
BF16 = jnp.bfloat16

D_MODEL = 2048
HEAD_DIM = 64
NSA_HEADS = 16
NSA_KV_HEADS = 4
NSA_GROUP = 4
SWA_HEADS = 16
SWA_KV_HEADS = 2
SWA_GROUP = 8
CMP_LEN = 32
CMP_STRIDE = 16
CMP_HIDDEN = 256
SEL_BLOCK = 64
SEL_SHIFT = 6
N_SEL = 16
NSA_WINDOW = 512
SWA_WINDOW = 128
NUM_BUCKETS = 32
REL_MAX_DISTANCE = 1024
RMS_EPS = 1e-5
NEG_INF = -1e30
FORCE_SCORE = 1e6
LOG2E = math.log2(math.e)
Q_SCALE = HEAD_DIM ** -0.5 * LOG2E

LANES = 128
MXU_DEPTH = 256
QB = 128
NQB = 256
KT = 128
SEL_TILE = 8 * KT
Q_PER_K = NQB // KT
C_PER_Q = NQB // CMP_STRIDE
N_NEAR = 8
N_WIN_TILES = NSA_WINDOW // KT + 1
BLK_LANES = LANES - HEAD_DIM
KEXT_W = BLK_LANES + HEAD_DIM
_MAX_EXACT = NUM_BUCKETS // 2
FAR_DIST = math.ceil(_MAX_EXACT * (REL_MAX_DISTANCE / _MAX_EXACT)
                     ** ((NUM_BUCKETS - _MAX_EXACT - 1) / (NUM_BUCKETS - _MAX_EXACT)))
assert KT * N_NEAR - (KT - 1) >= FAR_DIST
assert KEXT_W <= MXU_DEPTH
MASKED = NUM_BUCKETS
ROW_TILE = 512
FFN_ROW_TILE = 1024
FF_TILE = 256
VMEM_LIMIT = 56 * 1024 * 1024

COL_QA = (0, 1024)
COL_KC, COL_VC = (1024, 1280), (1280, 1536)
COL_KS, COL_VS = (1536, 1792), (1792, 2048)
COL_KW, COL_VW = (2048, 2304), (2304, 2560)
COL_GT = (2560, 2608)
COL_QB = (2608, 3632)
COL_KB, COL_VB = (3632, 3760), (3760, 3888)
GATE_ROWS = 16
N_GATES = NSA_KV_HEADS * GATE_ROWS
ROWMAJOR = (COL_KC, COL_VC, COL_KS, COL_KW, COL_KB)


def _dot(a, b):
    return jnp.dot(a, b, preferred_element_type=F32)


def _dot_nt(a, b):
    return lax.dot_general(a, b, (((1,), (1,)), ((), ())), preferred_element_type=F32)


def _rms(x, g):
    return x * lax.rsqrt(jnp.mean(x * x, axis=-1, keepdims=True) + RMS_EPS) * g


def _inproj_kernel(x_ref, g_ref, wt_ref, br_ref, bc_ref, wg_ref, bg_ref,
                   qat_ref, kc_ref, vc_ref, ks_ref, kw_ref, vst_ref, vwt_ref, gtt_ref,
                   qbt_ref, kb_ref, vbt_ref):
    h = _rms(x_ref[0], g_ref[...]).astype(BF16)

    def colmajor(cols):
        a, b = cols
        return _dot_nt(wt_ref[a:b, :], h) + bc_ref[a:b, :]

    off = 0
    for (a, b), ref in zip(ROWMAJOR, (kc_ref, vc_ref, ks_ref, kw_ref, kb_ref)):
        r = _dot_nt(h, wt_ref[a:b, :]) + br_ref[:, off:off + b - a]
        off += b - a
        for hh in range((b - a) // HEAD_DIM):
            ref[0, hh] = r[:, HEAD_DIM * hh:HEAD_DIM * (hh + 1)].astype(ref.dtype)
    qat_ref[0] = (colmajor(COL_QA) * Q_SCALE).astype(qat_ref.dtype)
    qbt_ref[0] = (colmajor(COL_QB) * Q_SCALE).astype(qbt_ref.dtype)
    vst_ref[0] = colmajor(COL_VS).astype(vst_ref.dtype)
    vwt_ref[0] = colmajor(COL_VW).astype(vwt_ref.dtype)
    vbt_ref[0] = colmajor(COL_VB).astype(vbt_ref.dtype)
    gtt_ref[0] = _dot_nt(wg_ref[...], h) + bg_ref[...]


def _inproj(x, g, wt, b_row, b_col, wg, bg, layer):
    bsz, t, d = x.shape
    tm = min(ROW_TILE, t)
    kv4 = lambda dt: jax.ShapeDtypeStruct((bsz, NSA_KV_HEADS, t, HEAD_DIM), dt)
    kv2 = jax.ShapeDtypeStruct((bsz, SWA_KV_HEADS, t, HEAD_DIM), BF16)
    colm = lambda rows, dt: jax.ShapeDtypeStruct((bsz, rows, t), dt)
    out_shape = (colm(1024, BF16), kv4(F32), kv4(F32), kv4(BF16), kv4(BF16),
                 colm(256, BF16), colm(256, BF16), colm(N_GATES, F32),
                 colm(1024, BF16), kv2, colm(128, BF16))
    kv4_spec = pl.BlockSpec((1, NSA_KV_HEADS, tm, HEAD_DIM), lambda bb, i: (bb, 0, i, 0))
    kv2_spec = pl.BlockSpec((1, SWA_KV_HEADS, tm, HEAD_DIM), lambda bb, i: (bb, 0, i, 0))
    colm_spec = lambda rows: pl.BlockSpec((1, rows, tm), lambda bb, i: (bb, 0, i))
    const = lambda a: pl.BlockSpec(a.shape, lambda bb, i: (0, 0), pipeline_mode=pl.Buffered(1))
    return pl.pallas_call(
        _inproj_kernel,
        grid=(bsz, t // tm),
        in_specs=[pl.BlockSpec((1, tm, d), lambda bb, i: (bb, i, 0)),
                  const(g),
                  pl.BlockSpec((None,) + wt.shape[1:], lambda bb, i: (layer, 0, 0), pipeline_mode=pl.Buffered(1)),
                  const(b_row), const(b_col), const(wg), const(bg)],
        out_specs=(colm_spec(1024), kv4_spec, kv4_spec, kv4_spec, kv4_spec,
                   colm_spec(256), colm_spec(256), colm_spec(N_GATES),
                   colm_spec(1024), kv2_spec, colm_spec(128)),
        out_shape=out_shape,
        compiler_params=pltpu.CompilerParams(
            dimension_semantics=("parallel", "parallel"), vmem_limit_bytes=VMEM_LIMIT),
        name="inproj",
    )(x, g, wt, b_row, b_col, wg, bg)


def _compress_kernel(kr_ref, vr_ref, pk_ref, pv_ref, k1_ref, k2_ref, v1_ref, v2t_ref,
                     ko_ref, vot_ref):
    n_chunk = kr_ref.shape[2] // CMP_STRIDE
    half = CMP_STRIDE * HEAD_DIM

    def hidden(r_ref, p_ref, w1_ref):
        top = jnp.zeros((n_chunk, CMP_HIDDEN), F32)
        bot = jnp.zeros((n_chunk, CMP_HIDDEN), F32)
        for l in range(CMP_STRIDE):
            xl = r_ref[0, 0, pl.ds(l, n_chunk, stride=CMP_STRIDE), :]
            a = (xl + p_ref[l:l + 1, :]).astype(BF16)
            top = top + _dot(a, w1_ref[HEAD_DIM * l:HEAD_DIM * (l + 1), :])
            c = (xl + p_ref[CMP_STRIDE + l:CMP_STRIDE + l + 1, :]).astype(BF16)
            bot = bot + _dot(c, w1_ref[half + HEAD_DIM * l:half + HEAD_DIM * (l + 1), :])
        hid = top + pltpu.roll(bot, n_chunk - 1, axis=0)
        return jax.nn.gelu(hid).astype(BF16)

    out = _dot(hidden(kr_ref, pk_ref, k1_ref), k2_ref[...])
    row = lax.broadcasted_iota(jnp.int32, out.shape, 0)
    ko_ref[0, 0] = jnp.where(row < n_chunk - 1, out, 0.0).astype(ko_ref.dtype)
    out_t = _dot_nt(v2t_ref[...], hidden(vr_ref, pv_ref, v1_ref))
    col = lax.broadcasted_iota(jnp.int32, out_t.shape, 1)
    vot_ref[0, 0] = jnp.where(col < n_chunk - 1, out_t, 0.0).astype(vot_ref.dtype)


def _compress(kc_raw, vc_raw, pos_k, pos_v, k1, k2, v1, v2t):
    bsz, hkv, t, dh = kc_raw.shape
    n_chunk = t // CMP_STRIDE
    raw_spec = pl.BlockSpec((1, 1, t, dh), lambda bb, hh: (bb, hh, 0, 0))
    full = lambda a: pl.BlockSpec(a.shape, lambda bb, hh: (0,) * a.ndim)
    return pl.pallas_call(
        _compress_kernel,
        grid=(bsz, hkv),
        in_specs=[raw_spec, raw_spec, full(pos_k), full(pos_v), full(k1), full(k2), full(v1), full(v2t)],
        out_specs=(pl.BlockSpec((1, 1, n_chunk, dh), lambda bb, hh: (bb, hh, 0, 0)),
                   pl.BlockSpec((1, 1, dh, n_chunk), lambda bb, hh: (bb, hh, 0, 0))),
        out_shape=(jax.ShapeDtypeStruct((bsz, hkv, n_chunk, dh), BF16),
                   jax.ShapeDtypeStruct((bsz, hkv, dh, n_chunk), BF16)),
        compiler_params=pltpu.CompilerParams(
            dimension_semantics=("parallel", "parallel"), vmem_limit_bytes=VMEM_LIMIT),
        name="compress",
    )(kc_raw, vc_raw, pos_k, pos_v, k1, k2, v1, v2t)


def _expand_kernel(tbl_ref, idx_ref, o_ref, *, gsz):
    grp = pl.program_id(0)
    idx = idx_ref[0]
    w = idx.shape[1]
    accs = [jnp.full(idx.shape, tbl_ref[MASKED, grp * gsz + g], F32) for g in range(gsz)]
    for b in range(NUM_BUCKETS):
        hit = idx == b
        for g in range(gsz):
            accs[g] = jnp.where(hit, tbl_ref[b, grp * gsz + g], accs[g])
    for g in range(gsz):
        o_ref[0, 0, :, w * g:w * (g + 1)] = accs[g]


def _expand_tables(tbl, idx, gsz):
    r0, r1, w = idx.shape
    heads = tbl.shape[1]
    tr = min(r1, KT)
    tbl_ext = jnp.concatenate([tbl.astype(F32) * LOG2E, jnp.full((1, heads), NEG_INF, F32)], axis=0)
    return pl.pallas_call(
        functools.partial(_expand_kernel, gsz=gsz),
        grid=(heads // gsz, r0, r1 // tr),
        in_specs=[pl.BlockSpec(memory_space=pltpu.SMEM),
                  pl.BlockSpec((1, tr, w), lambda gg, a, r: (a, r, 0))],
        out_specs=pl.BlockSpec((1, 1, tr, gsz * w), lambda gg, a, r: (gg, a, r, 0)),
        out_shape=jax.ShapeDtypeStruct((heads // gsz, r0, r1, gsz * w), F32),
        compiler_params=pltpu.CompilerParams(
            dimension_semantics=("parallel", "parallel", "parallel"), vmem_limit_bytes=VMEM_LIMIT),
        name="expand_bias",
    )(tbl_ext, idx)


def _t5_bucket(dist):
    dist = jnp.maximum(dist, 0)
    max_exact = NUM_BUCKETS // 2
    d = jnp.maximum(dist, 1).astype(F32)
    ratio = jnp.log(d / max_exact) / math.log(REL_MAX_DISTANCE / max_exact)
    large = max_exact + (ratio * (NUM_BUCKETS - max_exact)).astype(jnp.int32)
    large = jnp.minimum(large, NUM_BUCKETS - 1)
    return jnp.where(dist < max_exact, dist, large)


def _bias_indices(n_cmp):
    kj = jnp.arange(KT)[:, None]
    qq = jnp.arange(NQB)[None, :]
    d = jnp.arange(-Q_PER_K, N_NEAR + 1)[:, None, None]
    dist = KT * d + qq - kj
    sel_idx = jnp.where(dist >= 0, _t5_bucket(dist), MASKED)
    d = jnp.arange(-(Q_PER_K - 1), N_WIN_TILES + 1)[:, None, None]
    dist = KT * d + qq - kj
    win_idx = jnp.where((dist >= 0) & (dist < NSA_WINDOW), _t5_bucket(dist), MASKED)
    cprime = jnp.arange(2 * n_cmp)[:, None] - (n_cmp - C_PER_Q)
    dist = qq - CMP_STRIDE * cprime - (CMP_LEN - 1)
    cmp_idx = jnp.where(dist >= 0, _t5_bucket(dist), MASKED)[None]
    kr = jnp.arange(2 * KT)[:, None]
    db = jnp.arange(QB)[None, :] + KT - kr
    ok = (db >= 0) & (db < SWA_WINDOW)
    ok = jnp.stack([ok, ok & (kr >= KT)])
    swa_idx = jnp.where(ok, _t5_bucket(db)[None], MASKED)
    i32 = lambda a: a.astype(jnp.int32)
    return i32(sel_idx), i32(win_idx), i32(cmp_idx), i32(swa_idx)


def _softmax_pv(s_tiles, vt_tiles, ones):
    m = s_tiles[0].max(axis=0, keepdims=True)
    for s in s_tiles[1:]:
        m = jnp.maximum(m, s.max(axis=0, keepdims=True))
    acc = None
    for s, vt in zip(s_tiles, vt_tiles):
        p = jnp.exp2(s - m).astype(BF16)
        vx = jnp.concatenate([vt, ones[:, :vt.shape[1]]], axis=0)
        part = _dot(vx, p)
        acc = part if acc is None else acc + part
    return acc[:HEAD_DIM] * (1.0 / acc[HEAD_DIM:HEAD_DIM + 1])


def _nsa_kernel(qt_ref, gt_ref, kc_ref, vct_ref, ks_ref, kw_ref, vst_ref, vwt_ref, et_ref,
                bc_ref, sb_ref, wb_ref, ov_ref, o_ref,
                kext_scr, rhs_scr, m_scr, acc_scr):
    qi = pl.program_id(2)
    t0 = qi * NQB
    n_cmp = kc_ref.shape[2]
    n_blk = ks_ref.shape[2] // SEL_BLOCK
    grp = NSA_GROUP
    lanes = grp * NQB

    @pl.when(qi == 0)
    def _():
        kext_scr[:, 0:BLK_LANES] = et_ref[...]
        kext_scr[:, BLK_LANES:] = ks_ref[0, 0]

    qt = qt_ref[0]
    qst = jnp.concatenate([qt[HEAD_DIM * g:HEAD_DIM * (g + 1)] for g in range(grp)], axis=1)
    ones = jnp.ones((16, SEL_TILE), BF16)

    start = pl.multiple_of(n_cmp - C_PER_Q - C_PER_Q * qi, 8)
    s = _dot(kc_ref[0, 0], qst) + bc_ref[0, 0, pl.ds(start, n_cmp), :]
    e = jnp.exp2(s - s.max(axis=0, keepdims=True))
    tq = t0 + (lax.broadcasted_iota(jnp.int32, (1, lanes), 1) & (NQB - 1))
    inv = jnp.where(tq >= CMP_LEN - 1, 1.0 / e.sum(axis=0, keepdims=True), 0.0)
    p = e * inv
    o_c = _dot(vct_ref[0, 0], p.astype(BF16))

    s_tiles, vt_tiles = [], []
    n_win_entries = Q_PER_K - 1 + N_WIN_TILES
    for d in range(-(Q_PER_K - 1), N_WIN_TILES):
        jj = Q_PER_K * qi - d
        k0 = pl.multiple_of(jnp.maximum(jj, 0) * KT, KT)
        entry = jnp.where(jj >= 0, d + Q_PER_K - 1, n_win_entries)
        s_tiles.append(_dot(kw_ref[0, 0, pl.ds(k0, KT), :], qst) + wb_ref[0, entry])
        vt_tiles.append(vwt_ref[0, :, pl.ds(k0, KT)])
    o_w = _softmax_pv(s_tiles, vt_tiles, ones)

    p_sum = p[:, 0:NQB]
    for g in range(1, grp):
        p_sum = p_sum + p[:, NQB * g:NQB * (g + 1)]
    p_hi = p_sum.astype(BF16)
    p_lo = (p_sum - p_hi.astype(F32)).astype(BF16)
    imp = (_dot(ov_ref[...], p_hi) + _dot(ov_ref[...], p_lo))[:n_blk]
    jb = lax.broadcasted_iota(jnp.int32, (n_blk, NQB), 0)
    blk_t = (t0 + lax.broadcasted_iota(jnp.int32, (n_blk, NQB), 1)) >> SEL_SHIFT
    forced = (jb == 0) | (jb == blk_t) | (jb == blk_t - 1)
    val = jnp.where(forced, imp + FORCE_SCORE, imp)
    val = jnp.where(jb > blk_t, NEG_INF, val)
    n_sel = min(N_SEL, n_blk)
    row8 = lax.broadcasted_iota(jnp.int32, (8, NQB), 0)

    def drop_mask(nb):
        def fn(v):
            if nb <= n_sel:
                return jnp.zeros((n_blk, NQB), F32)
            vals = [v[8 * k:8 * (k + 1)] for k in range(nb // 8)]
            ranks = [jnp.zeros((8, NQB), jnp.int32) for _ in vals]
            for j in range(nb):
                r = jnp.broadcast_to(v[j:j + 1], (8, NQB))
                for k in range(nb // 8):
                    if 8 * k > j:
                        beats = r >= vals[k]
                    elif 8 * k + 7 <= j:
                        beats = r > vals[k]
                    else:
                        beats = (r > vals[k]) | ((r == vals[k]) & (row8 + 8 * k > j))
                    ranks[k] = ranks[k] + jnp.where(beats, 1, 0)
            out = [jnp.where(rk < n_sel, 0.0, NEG_INF) for rk in ranks]
            return jnp.concatenate(out + [jnp.zeros((n_blk - nb, NQB), F32)] * (nb < n_blk), axis=0)
        return fn

    visible = (NQB // SEL_BLOCK) * (qi + 1)
    prefixes = list(range(n_sel, n_blk, n_sel)) + [n_blk]
    choose = drop_mask(prefixes[-1])
    for nb in reversed(prefixes[:-1]):
        choose = functools.partial(
            lambda v, nb, rest: lax.cond(visible <= nb, drop_mask(nb), rest, v), nb=nb, rest=choose)
    drop = choose(val)
    drop = jnp.concatenate([drop] + [jnp.zeros((BLK_LANES - n_blk, NQB), F32)] * (n_blk < BLK_LANES),
                           axis=0).astype(BF16)
    rhs_scr[...] = jnp.concatenate([jnp.concatenate([drop] * grp, axis=1), qst], axis=0)

    tk = SEL_TILE
    m_scr[...] = jnp.full((1, lanes), NEG_INF, F32)
    acc_scr[...] = jnp.zeros(acc_scr.shape, F32)

    def sel_body(i, carry):
        k0 = pl.multiple_of(i * tk, tk)
        s = _dot(kext_scr[pl.ds(k0, tk), :], rhs_scr[...])
        d = Q_PER_K * qi - (tk // KT) * i
        entries = [jnp.clip(d - c, -Q_PER_K, N_NEAR) + Q_PER_K for c in range(tk // KT)]
        s = s + jnp.concatenate([sb_ref[0, e] for e in entries], axis=0)
        m_prev = m_scr[...]
        m_new = jnp.maximum(m_prev, s.max(axis=0, keepdims=True))
        p = jnp.exp2(s - m_new).astype(BF16)
        vx = jnp.concatenate([vst_ref[0, :, pl.ds(k0, tk)], ones], axis=0)
        acc_scr[...] = acc_scr[...] * jnp.exp2(m_prev - m_new) + _dot(vx, p)
        m_scr[...] = m_new
        return carry

    n_tiles = (KT * Q_PER_K * (qi + 1) + tk - 1) // tk
    lax.fori_loop(0, n_tiles, sel_body, 0)
    acc = acc_scr[...]
    o_s = acc[:HEAD_DIM] * (1.0 / acc[HEAD_DIM:HEAD_DIM + 1])

    gate = jax.nn.sigmoid(gt_ref[0])
    outs = []
    for g in range(grp):
        sl = slice(NQB * g, NQB * (g + 1))
        outs.append(gate[3 * g:3 * g + 1] * o_c[:, sl] + gate[3 * g + 1:3 * g + 2] * o_s[:, sl]
                    + gate[3 * g + 2:3 * g + 3] * o_w[:, sl])
    o_ref[0] = jnp.concatenate(outs, axis=0).T.astype(o_ref.dtype)


def _nsa_attention(qat, gtt, kc, vct, ks, kw, vst, vwt, et, bct, sbt, wbt, ovt):
    bsz, _, t = qat.shape
    hkv = NSA_KV_HEADS
    n_cmp = kc.shape[2]
    gw = NSA_GROUP * HEAD_DIM
    lanes = NSA_GROUP * NQB
    kv_spec = pl.BlockSpec((1, 1, t, HEAD_DIM), lambda hh, bb, i: (bb, hh, 0, 0))
    vt_spec = pl.BlockSpec((1, HEAD_DIM, t), lambda hh, bb, i: (bb, hh, 0))
    tbl = lambda a: pl.BlockSpec((1,) + a.shape[1:], lambda hh, bb, i: (hh, 0, 0, 0))
    const = lambda a: pl.BlockSpec(a.shape, lambda hh, bb, i: (0, 0))
    return pl.pallas_call(
        _nsa_kernel,
        grid=(hkv, bsz, t // NQB),
        in_specs=[
            pl.BlockSpec((1, gw, NQB), lambda hh, bb, i: (bb, hh, i)),
            pl.BlockSpec((1, GATE_ROWS, NQB), lambda hh, bb, i: (bb, hh, i)),
            pl.BlockSpec((1, 1, n_cmp, HEAD_DIM), lambda hh, bb, i: (bb, hh, 0, 0)),
            pl.BlockSpec((1, 1, HEAD_DIM, n_cmp), lambda hh, bb, i: (bb, hh, 0, 0)),
            kv_spec, kv_spec, vt_spec, vt_spec, const(et),
            tbl(bct), tbl(sbt), tbl(wbt), const(ovt),
        ],
        out_specs=pl.BlockSpec((1, NQB, gw), lambda hh, bb, i: (bb, i, hh)),
        out_shape=jax.ShapeDtypeStruct((bsz, t, NSA_HEADS * HEAD_DIM), BF16),
        scratch_shapes=[pltpu.VMEM((t, KEXT_W), BF16),
                        pltpu.VMEM((KEXT_W, lanes), BF16),
                        pltpu.VMEM((1, lanes), F32),
                        pltpu.VMEM((HEAD_DIM + 16, lanes), F32)],
        compiler_params=pltpu.CompilerParams(
            dimension_semantics=("parallel", "parallel", "arbitrary"), vmem_limit_bytes=VMEM_LIMIT),
        name="nsa_attention",
    )(qat, gtt, kc, vct, ks, kw, vst, vwt, et, bct, sbt, wbt, ovt)


def _swa_kernel(sink_ref, qt_ref, kp_ref, kn_ref, vp_ref, vn_ref, wb_ref, o_ref):
    qi = pl.program_id(2)
    qt = qt_ref[0]
    qst = jnp.concatenate([qt[HEAD_DIM * g:HEAD_DIM * (g + 1)] for g in range(SWA_GROUP)], axis=1)
    k = jnp.concatenate([kp_ref[0, 0], kn_ref[0, 0]], axis=0)
    vt = jnp.concatenate([vp_ref[0], vn_ref[0]], axis=1)
    s = _dot(k, qst) + wb_ref[0, jnp.where(qi == 0, 1, 0)]
    sink = sink_ref[0]
    m = jnp.maximum(s.max(axis=0, keepdims=True), sink)
    p = jnp.exp2(s - m).astype(BF16)
    vx = jnp.concatenate([vt, jnp.ones((16, 2 * KT), BF16)], axis=0)
    acc = _dot(vx, p)
    o = acc[:HEAD_DIM] * (1.0 / (acc[HEAD_DIM:HEAD_DIM + 1] + jnp.exp2(sink - m)))
    o = jnp.concatenate([o[:, QB * g:QB * (g + 1)] for g in range(SWA_GROUP)], axis=0)
    o_ref[0] = o.T.astype(o_ref.dtype)


def _swa_attention(sink_rows, qbt, kb, vbt, wbb):
    bsz, _, t = qbt.shape
    gw = SWA_GROUP * HEAD_DIM
    lanes = SWA_GROUP * QB
    kprev = pl.BlockSpec((1, 1, KT, HEAD_DIM), lambda bb, hh, i: (bb, hh, jnp.maximum(i - 1, 0), 0))
    kcur = pl.BlockSpec((1, 1, KT, HEAD_DIM), lambda bb, hh, i: (bb, hh, i, 0))
    vprev = pl.BlockSpec((1, HEAD_DIM, KT), lambda bb, hh, i: (bb, hh, jnp.maximum(i - 1, 0)))
    vcur = pl.BlockSpec((1, HEAD_DIM, KT), lambda bb, hh, i: (bb, hh, i))
    return pl.pallas_call(
        _swa_kernel,
        grid=(bsz, SWA_KV_HEADS, t // QB),
        in_specs=[
            pl.BlockSpec((1, 1, lanes), lambda bb, hh, i: (hh, 0, 0)),
            pl.BlockSpec((1, gw, QB), lambda bb, hh, i: (bb, hh, i)),
            kprev, kcur, vprev, vcur,
            pl.BlockSpec((1,) + wbb.shape[1:], lambda bb, hh, i: (hh, 0, 0, 0)),
        ],
        out_specs=pl.BlockSpec((1, QB, gw), lambda bb, hh, i: (bb, i, hh)),
        out_shape=jax.ShapeDtypeStruct((bsz, t, SWA_HEADS * HEAD_DIM), BF16),
        compiler_params=pltpu.CompilerParams(
            dimension_semantics=("parallel", "parallel", "arbitrary"), vmem_limit_bytes=VMEM_LIMIT),
        name="swa_attention",
    )(sink_rows, qbt, kb, kb, vbt, vbt, wbb)


def _outproj_kernel(x_ref, oa_ref, ob_ref, w_ref, o_ref):
    na = oa_ref.shape[1]
    o_ref[...] = x_ref[...] + _dot(oa_ref[...], w_ref[:na, :]) + _dot(ob_ref[...], w_ref[na:, :])


def _outproj(x, oa, ob, w, layer):
    n, d = x.shape
    tm = min(ROW_TILE, n)
    row = lambda width: pl.BlockSpec((tm, width), lambda i: (i, 0))
    return pl.pallas_call(
        _outproj_kernel,
        grid=(n // tm,),
        in_specs=[row(d), row(oa.shape[1]), row(ob.shape[1]),
                  pl.BlockSpec((None,) + w.shape[1:], lambda i: (layer, 0, 0))],
        out_specs=row(d),
        out_shape=jax.ShapeDtypeStruct((n, d), F32),
        compiler_params=pltpu.CompilerParams(
            dimension_semantics=("parallel",), vmem_limit_bytes=VMEM_LIMIT),
        name="outproj",
    )(x, oa, ob, w)


def _ffn_kernel(x_ref, g_ref, wg_ref, wu_ref, wd_ref, gf_ref, o_ref, h_scr, *, final_norm):
    f = pl.program_id(1)

    @pl.when(f == 0)
    def _():
        x = x_ref[...]
        h_scr[...] = _rms(x, g_ref[...]).astype(BF16)
        o_ref[...] = x

    h = h_scr[...]
    gate = _dot(h, wg_ref[...].astype(BF16))
    act = (gate * jax.nn.sigmoid(gate) * _dot(h, wu_ref[...].astype(BF16))).astype(BF16)
    half = o_ref.shape[1] // 2
    o_ref[:, :half] += _dot(act, wd_ref[:, :half].astype(BF16))
    o_ref[:, half:] += _dot(act, wd_ref[:, half:].astype(BF16))

    if final_norm:
        @pl.when(f == pl.num_programs(1) - 1)
        def _():
            o_ref[...] = _rms(o_ref[...], gf_ref[...])


def _ffn(x, g, wg, wu, wd, g_final, final_norm, layer):
    n, d = x.shape
    dff = wg.shape[2]
    tm = min(FFN_ROW_TILE, n)
    return pl.pallas_call(
        functools.partial(_ffn_kernel, final_norm=final_norm),
        grid=(n // tm, dff // FF_TILE),
        in_specs=[
            pl.BlockSpec((tm, d), lambda i, f: (i, 0)),
            pl.BlockSpec((1, d), lambda i, f: (0, 0)),
            pl.BlockSpec((None, d, FF_TILE), lambda i, f: (layer, 0, f)),
            pl.BlockSpec((None, d, FF_TILE), lambda i, f: (layer, 0, f)),
            pl.BlockSpec((None, FF_TILE, d), lambda i, f: (layer, f, 0)),
            pl.BlockSpec((1, d), lambda i, f: (0, 0)),
        ],
        out_specs=pl.BlockSpec((tm, d), lambda i, f: (i, 0)),
        out_shape=jax.ShapeDtypeStruct((n, d), F32),
        scratch_shapes=[pltpu.VMEM((tm, d), BF16)],
        compiler_params=pltpu.CompilerParams(
            dimension_semantics=("parallel", "arbitrary"), vmem_limit_bytes=VMEM_LIMIT),
        name="ffn",
    )(x, g, wg, wu, wd, g_final)


def _gate_rows(a):
    depth, _, n = a.shape
    a = a.reshape(depth, NSA_KV_HEADS, 3 * NSA_GROUP, n)
    a = jnp.pad(a, ((0, 0), (0, 0), (0, GATE_ROWS - 3 * NSA_GROUP), (0, 0)))
    return a.reshape(depth, N_GATES, n)


def _overlap_t(n_cmp, n_blk):
    ci = jnp.arange(n_cmp)[None, :] * CMP_STRIDE
    blk = jnp.arange(BLK_LANES)[:, None]
    sj = blk * SEL_BLOCK
    ov = (ci < sj + SEL_BLOCK) & (ci + CMP_LEN > sj) & (blk < n_blk)
    return (ov & (jnp.arange(n_cmp)[None, :] < n_cmp - 1)).astype(BF16)


def kernel(x, rel_bias, norm_mix, norm_ffn, w_in, b_in, cmp_pos_k, cmp_pos_v, cmp_k_w1, cmp_k_w2,
           cmp_v_w1, cmp_v_w2, sinks, w_out, w_gate, w_up, w_down, norm_final):
    bsz, t, d = x.shape
    depth = w_in.shape[0]
    assert d == D_MODEL and w_in.shape[2] == COL_VB[1] and rel_bias.shape == (NUM_BUCKETS, NSA_HEADS + SWA_HEADS)
    assert t % SEL_TILE == 0 and t // SEL_BLOCK <= BLK_LANES
    assert (bsz * t) % FFN_ROW_TILE == 0 and w_gate.shape[2] % FF_TILE == 0
    n_cmp = t // CMP_STRIDE
    sel_idx, win_idx, cmp_idx, swa_idx = _bias_indices(n_cmp)
    tbl_a, tbl_b = rel_bias[:, :NSA_HEADS], rel_bias[:, NSA_HEADS:]
    sbt = _expand_tables(tbl_a, sel_idx, NSA_GROUP)
    wbt = _expand_tables(tbl_a, win_idx, NSA_GROUP)
    bct = _expand_tables(tbl_a, cmp_idx, NSA_GROUP)
    wbb = _expand_tables(tbl_b, swa_idx, SWA_GROUP)
    ovt = _overlap_t(n_cmp, t // SEL_BLOCK)
    et = (jnp.arange(t)[:, None] // SEL_BLOCK == jnp.arange(BLK_LANES)[None, :]).astype(BF16)
    w_in_t = jnp.swapaxes(w_in.astype(BF16), -1, -2)
    b_row = jnp.concatenate([b_in[:, a:b] for a, b in ROWMAJOR], axis=-1).astype(F32)
    w_gt, b_gt = _gate_rows(w_in_t[:, COL_GT[0]:COL_GT[1], :]), _gate_rows(b_in[:, COL_GT[0]:COL_GT[1], None])
    w_out16 = w_out.astype(BF16)
    for layer in range(depth):
        qat, kc_raw, vc_raw, ks, kw, vst, vwt, gtt, qbt, kb, vbt = _inproj(
            x, norm_mix[layer][None, :], w_in_t, b_row[layer][None, :],
            b_in[layer].astype(F32)[:, None], w_gt[layer], b_gt[layer].astype(F32), layer)
        kc, vct = _compress(kc_raw, vc_raw, cmp_pos_k[layer], cmp_pos_v[layer],
                            cmp_k_w1[layer].astype(BF16), cmp_k_w2[layer].astype(BF16),
                            cmp_v_w1[layer].astype(BF16), cmp_v_w2[layer].T.astype(BF16))
        o_a = _nsa_attention(qat, gtt, kc, vct, ks, kw, vst, vwt, et, bct, sbt, wbt, ovt)
        sink_rows = jnp.repeat(sinks[layer].astype(F32) * LOG2E, QB).reshape(SWA_KV_HEADS, 1, SWA_GROUP * QB)
        o_b = _swa_attention(sink_rows, qbt, kb, vbt, wbb)
        x2 = _outproj(x.reshape(bsz * t, d), o_a.reshape(bsz * t, -1), o_b.reshape(bsz * t, -1),
                      w_out16, layer)
        x2 = _ffn(x2, norm_ffn[layer][None, :], w_gate, w_up, w_down, norm_final[None, :],
                  layer == depth - 1, layer)
        x = x2.reshape(bsz, t, d)
    return x
```

```python
import functools
import math

import jax
import jax.numpy as jnp
from jax import lax
from jax.experimental import pallas as pl
from jax.experimental.pallas import tpu as pltpu

F32 = jnp.float32
BF16 = jnp.bfloat16

D_MODEL = 2048
HEAD_DIM = 64
NSA_HEADS = 16
NSA_KV_HEADS = 4
NSA_GROUP = 4
SWA_HEADS = 16
SWA_KV_HEADS = 2
SWA_GROUP = 8
CMP_LEN = 32
CMP_STRIDE = 16
CMP_HIDDEN = 256
SEL_BLOCK = 64
SEL_SHIFT = 6
N_SEL = 16
NSA_WINDOW = 512
SWA_WINDOW = 128
NUM_BUCKETS = 32
REL_MAX_DISTANCE = 1024
RMS_EPS = 1e-5
NEG_INF = -1e30
FORCE_SCORE = 1e6
LOG2E = math.log2(math.e)
Q_SCALE = HEAD_DIM ** -0.5 * LOG2E

LANES = 128
MXU_DEPTH = 256
QB = 128
NQB = 256
KT = 128
SEL_TILE = 8 * KT
Q_PER_K = NQB // KT
C_PER_Q = NQB // CMP_STRIDE
N_NEAR = 8
N_WIN_TILES = NSA_WINDOW // KT + 1
BLK_LANES = LANES - HEAD_DIM
KEXT_W = BLK_LANES + HEAD_DIM
_MAX_EXACT = NUM_BUCKETS // 2
FAR_DIST = math.ceil(_MAX_EXACT * (REL_MAX_DISTANCE / _MAX_EXACT)
                     ** ((NUM_BUCKETS - _MAX_EXACT - 1) / (NUM_BUCKETS - _MAX_EXACT)))
assert KT * N_NEAR - (KT - 1) >= FAR_DIST
assert KEXT_W <= MXU_DEPTH
MASKED = NUM_BUCKETS
ROW_TILE = 512
FFN_ROW_TILE = 1024
FF_TILE = 256
VMEM_LIMIT = 56 * 1024 * 1024

COL_QA = (0, 1024)
COL_KC, COL_VC = (1024, 1280), (1280, 1536)
COL_KS, COL_VS = (1536, 1792), (1792, 2048)
COL_KW, COL_VW = (2048, 2304), (2304, 2560)
COL_GT = (2560, 2608)
COL_QB = (2608, 3632)
COL_KB, COL_VB = (3632, 3760), (3760, 3888)
GATE_ROWS = 16
N_GATES = NSA_KV_HEADS * GATE_ROWS
ROWMAJOR = (COL_KC, COL_VC, COL_KS, COL_KW, COL_KB)


def _dot(a, b):
    return jnp.dot(a, b, preferred_element_type=F32)


def _dot_nt(a, b):
    return lax.dot_general(a, b, (((1,), (1,)), ((), ())), preferred_element_type=F32)


def _rms(x, g):
    return x * lax.rsqrt(jnp.mean(x * x, axis=-1, keepdims=True) + RMS_EPS) * g


def _inproj_kernel(x_ref, g_ref, wt_ref, br_ref, bc_ref, wg_ref, bg_ref,
                   qat_ref, kc_ref, vc_ref, ks_ref, kw_ref, vst_ref, vwt_ref, gtt_ref,
                   qbt_ref, kb_ref, vbt_ref):
    h = _rms(x_ref[0], g_ref[...]).astype(BF16)

    def colmajor(cols):
        a, b = cols
        return _dot_nt(wt_ref[a:b, :], h) + bc_ref[a:b, :]

    off = 0
    for (a, b), ref in zip(ROWMAJOR, (kc_ref, vc_ref, ks_ref, kw_ref, kb_ref)):
        r = _dot_nt(h, wt_ref[a:b, :]) + br_ref[:, off:off + b - a]
        off += b - a
        for hh in range((b - a) // HEAD_DIM):
            ref[0, hh] = r[:, HEAD_DIM * hh:HEAD_DIM * (hh + 1)].astype(ref.dtype)
    qat_ref[0] = (colmajor(COL_QA) * Q_SCALE).astype(qat_ref.dtype)
    qbt_ref[0] = (colmajor(COL_QB) * Q_SCALE).astype(qbt_ref.dtype)
    vst_ref[0] = colmajor(COL_VS).astype(vst_ref.dtype)
    vwt_ref[0] = colmajor(COL_VW).astype(vwt_ref.dtype)
    vbt_ref[0] = colmajor(COL_VB).astype(vbt_ref.dtype)
    gtt_ref[0] = _dot_nt(wg_ref[...], h) + bg_ref[...]


def _inproj(x, g, wt, b_row, b_col, wg, bg, layer):
    bsz, t, d = x.shape
    tm = min(ROW_TILE, t)
    kv4 = lambda dt: jax.ShapeDtypeStruct((bsz, NSA_KV_HEADS, t, HEAD_DIM), dt)
    kv2 = jax.ShapeDtypeStruct((bsz, SWA_KV_HEADS, t, HEAD_DIM), BF16)
    colm = lambda rows, dt: jax.ShapeDtypeStruct((bsz, rows, t), dt)
    out_shape = (colm(1024, BF16), kv4(F32), kv4(F32), kv4(BF16), kv4(BF16),
                 colm(256, BF16), colm(256, BF16), colm(N_GATES, F32),
                 colm(1024, BF16), kv2, colm(128, BF16))
    kv4_spec = pl.BlockSpec((1, NSA_KV_HEADS, tm, HEAD_DIM), lambda bb, i: (bb, 0, i, 0))
    kv2_spec = pl.BlockSpec((1, SWA_KV_HEADS, tm, HEAD_DIM), lambda bb, i: (bb, 0, i, 0))
    colm_spec = lambda rows: pl.BlockSpec((1, rows, tm), lambda bb, i: (bb, 0, i))
    const = lambda a: pl.BlockSpec(a.shape, lambda bb, i: (0, 0), pipeline_mode=pl.Buffered(1))
    return pl.pallas_call(
        _inproj_kernel,
        grid=(bsz, t // tm),
        in_specs=[pl.BlockSpec((1, tm, d), lambda bb, i: (bb, i, 0)),
                  const(g),
                  pl.BlockSpec((None,) + wt.shape[1:], lambda bb, i: (layer, 0, 0), pipeline_mode=pl.Buffered(1)),
                  const(b_row), const(b_col), const(wg), const(bg)],
        out_specs=(colm_spec(1024), kv4_spec, kv4_spec, kv4_spec, kv4_spec,
                   colm_spec(256), colm_spec(256), colm_spec(N_GATES),
                   colm_spec(1024), kv2_spec, colm_spec(128)),
        out_shape=out_shape,
        compiler_params=pltpu.CompilerParams(
            dimension_semantics=("parallel", "parallel"), vmem_limit_bytes=VMEM_LIMIT),
        name="inproj",
    )(x, g, wt, b_row, b_col, wg, bg)


def _compress_kernel(kr_ref, vr_ref, pk_ref, pv_ref, k1_ref, k2_ref, v1_ref, v2t_ref,
                     ko_ref, vot_ref):
    n_chunk = kr_ref.shape[2] // CMP_STRIDE
    half = CMP_STRIDE * HEAD_DIM

    def hidden(r_ref, p_ref, w1_ref):
        top = jnp.zeros((n_chunk, CMP_HIDDEN), F32)
        bot = jnp.zeros((n_chunk, CMP_HIDDEN), F32)
        for l in range(CMP_STRIDE):
            xl = r_ref[0, 0, pl.ds(l, n_chunk, stride=CMP_STRIDE), :]
            a = (xl + p_ref[l:l + 1, :]).astype(BF16)
            top = top + _dot(a, w1_ref[HEAD_DIM * l:HEAD_DIM * (l + 1), :])
            c = (xl + p_ref[CMP_STRIDE + l:CMP_STRIDE + l + 1, :]).astype(BF16)
            bot = bot + _dot(c, w1_ref[half + HEAD_DIM * l:half + HEAD_DIM * (l + 1), :])
        hid = top + pltpu.roll(bot, n_chunk - 1, axis=0)
        return jax.nn.gelu(hid).astype(BF16)

    out = _dot(hidden(kr_ref, pk_ref, k1_ref), k2_ref[...])
    row = lax.broadcasted_iota(jnp.int32, out.shape, 0)
    ko_ref[0, 0] = jnp.where(row < n_chunk - 1, out, 0.0).astype(ko_ref.dtype)
    out_t = _dot_nt(v2t_ref[...], hidden(vr_ref, pv_ref, v1_ref))
    col = lax.broadcasted_iota(jnp.int32, out_t.shape, 1)
    vot_ref[0, 0] = jnp.where(col < n_chunk - 1, out_t, 0.0).astype(vot_ref.dtype)


def _compress(kc_raw, vc_raw, pos_k, pos_v, k1, k2, v1, v2t):
    bsz, hkv, t, dh = kc_raw.shape
    n_chunk = t // CMP_STRIDE
    raw_spec = pl.BlockSpec((1, 1, t, dh), lambda bb, hh: (bb, hh, 0, 0))
    full = lambda a: pl.BlockSpec(a.shape, lambda bb, hh: (0,) * a.ndim)
    return pl.pallas_call(
        _compress_kernel,
        grid=(bsz, hkv),
        in_specs=[raw_spec, raw_spec, full(pos_k), full(pos_v), full(k1), full(k2), full(v1), full(v2t)],
        out_specs=(pl.BlockSpec((1, 1, n_chunk, dh), lambda bb, hh: (bb, hh, 0, 0)),
                   pl.BlockSpec((1, 1, dh, n_chunk), lambda bb, hh: (bb, hh, 0, 0))),
        out_shape=(jax.ShapeDtypeStruct((bsz, hkv, n_chunk, dh), BF16),
                   jax.ShapeDtypeStruct((bsz, hkv, dh, n_chunk), BF16)),
        compiler_params=pltpu.CompilerParams(
            dimension_semantics=("parallel", "parallel"), vmem_limit_bytes=VMEM_LIMIT),
        name="compress",
    )(kc_raw, vc_raw, pos_k, pos_v, k1, k2, v1, v2t)


def _expand_kernel(tbl_ref, idx_ref, o_ref, *, gsz):
    grp = pl.program_id(0)
    idx = idx_ref[0]
    w = idx.shape[1]
    accs = [jnp.full(idx.shape, tbl_ref[MASKED, grp * gsz + g], F32) for g in range(gsz)]
    for b in range(NUM_BUCKETS):
        hit = idx == b
        for g in range(gsz):
            accs[g] = jnp.where(hit, tbl_ref[b, grp * gsz + g], accs[g])
    for g in range(gsz):
        o_ref[0, 0, :, w * g:w * (g + 1)] = accs[g]


def _expand_tables(tbl, idx, gsz):
    r0, r1, w = idx.shape
    heads = tbl.shape[1]
    tr = min(r1, KT)
    tbl_ext = jnp.concatenate([tbl.astype(F32) * LOG2E, jnp.full((1, heads), NEG_INF, F32)], axis=0)
    return pl.pallas_call(
        functools.partial(_expand_kernel, gsz=gsz),
        grid=(heads // gsz, r0, r1 // tr),
        in_specs=[pl.BlockSpec(memory_space=pltpu.SMEM),
                  pl.BlockSpec((1, tr, w), lambda gg, a, r: (a, r, 0))],
        out_specs=pl.BlockSpec((1, 1, tr, gsz * w), lambda gg, a, r: (gg, a, r, 0)),
        out_shape=jax.ShapeDtypeStruct((heads // gsz, r0, r1, gsz * w), F32),
        compiler_params=pltpu.CompilerParams(
            dimension_semantics=("parallel", "parallel", "parallel"), vmem_limit_bytes=VMEM_LIMIT),
        name="expand_bias",
    )(tbl_ext, idx)


def _t5_bucket(dist):
    dist = jnp.maximum(dist, 0)
    max_exact = NUM_BUCKETS // 2
    d = jnp.maximum(dist, 1).astype(F32)
    ratio = jnp.log(d / max_exact) / math.log(REL_MAX_DISTANCE / max_exact)
    large = max_exact + (ratio * (NUM_BUCKETS - max_exact)).astype(jnp.int32)
    large = jnp.minimum(large, NUM_BUCKETS - 1)
    return jnp.where(dist < max_exact, dist, large)


def _bias_indices(n_cmp):
    kj = jnp.arange(KT)[:, None]
    qq = jnp.arange(NQB)[None, :]
    d = jnp.arange(-Q_PER_K, N_NEAR + 1)[:, None, None]
    dist = KT * d + qq - kj
    sel_idx = jnp.where(dist >= 0, _t5_bucket(dist), MASKED)
    d = jnp.arange(-(Q_PER_K - 1), N_WIN_TILES + 1)[:, None, None]
    dist = KT * d + qq - kj
    win_idx = jnp.where((dist >= 0) & (dist < NSA_WINDOW), _t5_bucket(dist), MASKED)
    cprime = jnp.arange(2 * n_cmp)[:, None] - (n_cmp - C_PER_Q)
    dist = qq - CMP_STRIDE * cprime - (CMP_LEN - 1)
    cmp_idx = jnp.where(dist >= 0, _t5_bucket(dist), MASKED)[None]
    kr = jnp.arange(2 * KT)[:, None]
    db = jnp.arange(QB)[None, :] + KT - kr
    ok = (db >= 0) & (db < SWA_WINDOW)
    ok = jnp.stack([ok, ok & (kr >= KT)])
    swa_idx = jnp.where(ok, _t5_bucket(db)[None], MASKED)
    i32 = lambda a: a.astype(jnp.int32)
    return i32(sel_idx), i32(win_idx), i32(cmp_idx), i32(swa_idx)


def _softmax_pv(s_tiles, vt_tiles, ones):
    m = s_tiles[0].max(axis=0, keepdims=True)
    for s in s_tiles[1:]:
        m = jnp.maximum(m, s.max(axis=0, keepdims=True))
    acc = None
    for s, vt in zip(s_tiles, vt_tiles):
        p = jnp.exp2(s - m).astype(BF16)
        vx = jnp.concatenate([vt, ones[:, :vt.shape[1]]], axis=0)
        part = _dot(vx, p)
        acc = part if acc is None else acc + part
    return acc[:HEAD_DIM] * (1.0 / acc[HEAD_DIM:HEAD_DIM + 1])


def _nsa_kernel(qt_ref, gt_ref, kc_ref, vct_ref, ks_ref, kw_ref, vst_ref, vwt_ref, et_ref,
                bc_ref, sb_ref, wb_ref, ov_ref, o_ref,
                kext_scr, rhs_scr, m_scr, acc_scr):
    qi = pl.program_id(2)
    t0 = qi * NQB
    n_cmp = kc_ref.shape[2]
    n_blk = ks_ref.shape[2] // SEL_BLOCK
    grp = NSA_GROUP
    lanes = grp * NQB

    @pl.when(qi == 0)
    def _():
        kext_scr[:, 0:BLK_LANES] = et_ref[...]
        kext_scr[:, BLK_LANES:] = ks_ref[0, 0]

    qt = qt_ref[0]
    qst = jnp.concatenate([qt[HEAD_DIM * g:HEAD_DIM * (g + 1)] for g in range(grp)], axis=1)
    ones = jnp.ones((16, SEL_TILE), BF16)

    start = pl.multiple_of(n_cmp - C_PER_Q - C_PER_Q * qi, 8)
    s = _dot(kc_ref[0, 0], qst) + bc_ref[0, 0, pl.ds(start, n_cmp), :]
    e = jnp.exp2(s - s.max(axis=0, keepdims=True))
    tq = t0 + (lax.broadcasted_iota(jnp.int32, (1, lanes), 1) & (NQB - 1))
    inv = jnp.where(tq >= CMP_LEN - 1, 1.0 / e.sum(axis=0, keepdims=True), 0.0)
    p = e * inv
    o_c = _dot(vct_ref[0, 0], p.astype(BF16))

    n_win_entries = Q_PER_K - 1 + N_WIN_TILES
    o_w_parts = []
    for part in range(Q_PER_K):
        cols = [slice(NQB * g + KT * part, NQB * g + KT * (part + 1)) for g in range(grp)]
        q_part = jnp.concatenate([qst[:, c] for c in cols], axis=1)
        s_tiles, vt_tiles = [], []
        for d in range(-part, N_WIN_TILES - part):
            jj = Q_PER_K * qi - d
            k0 = pl.multiple_of(jnp.maximum(jj, 0) * KT, KT)
            entry = jnp.where(jj >= 0, d + Q_PER_K - 1, n_win_entries)
            bias = jnp.concatenate([wb_ref[0, entry, :, c] for c in cols], axis=1)
            s_tiles.append(_dot(kw_ref[0, 0, pl.ds(k0, KT), :], q_part) + bias)
            vt_tiles.append(vwt_ref[0, :, pl.ds(k0, KT)])
        o_w_parts.append(_softmax_pv(s_tiles, vt_tiles, ones))
    o_w = jnp.concatenate([o_w_parts[part][:, KT * g:KT * (g + 1)]
                           for g in range(grp) for part in range(Q_PER_K)], axis=1)

    p_sum = p[:, 0:NQB]
    for g in range(1, grp):
        p_sum = p_sum + p[:, NQB * g:NQB * (g + 1)]
    p_hi = p_sum.astype(BF16)
    p_lo = (p_sum - p_hi.astype(F32)).astype(BF16)
    imp = (_dot(ov_ref[...], p_hi) + _dot(ov_ref[...], p_lo))[:n_blk]
    jb = lax.broadcasted_iota(jnp.int32, (n_blk, NQB), 0)
    blk_t = (t0 + lax.broadcasted_iota(jnp.int32, (n_blk, NQB), 1)) >> SEL_SHIFT
    forced = (jb == 0) | (jb == blk_t) | (jb == blk_t - 1)
    val = jnp.where(forced, imp + FORCE_SCORE, imp)
    val = jnp.where(jb > blk_t, NEG_INF, val)
    n_sel = min(N_SEL, n_blk)
    row8 = lax.broadcasted_iota(jnp.int32, (8, NQB), 0)

    def drop_mask(nb):
        def fn(v):
            if nb <= n_sel:
                return jnp.zeros((n_blk, NQB), F32)
            vals = [v[8 * k:8 * (k + 1)] for k in range(nb // 8)]
            ranks = [jnp.zeros((8, NQB), jnp.int32) for _ in vals]
            for j in range(nb):
                r = jnp.broadcast_to(v[j:j + 1], (8, NQB))
                for k in range(nb // 8):
                    if 8 * k > j:
                        beats = r >= vals[k]
                    elif 8 * k + 7 <= j:
                        beats = r > vals[k]
                    else:
                        beats = (r > vals[k]) | ((r == vals[k]) & (row8 + 8 * k > j))
                    ranks[k] = ranks[k] + jnp.where(beats, 1, 0)
            out = [jnp.where(rk < n_sel, 0.0, NEG_INF) for rk in ranks]
            return jnp.concatenate(out + [jnp.zeros((n_blk - nb, NQB), F32)] * (nb < n_blk), axis=0)
        return fn

    visible = (NQB // SEL_BLOCK) * (qi + 1)
    prefixes = list(range(n_sel, n_blk, n_sel)) + [n_blk]
    choose = drop_mask(prefixes[-1])
    for nb in reversed(prefixes[:-1]):
        choose = functools.partial(
            lambda v, nb, rest: lax.cond(visible <= nb, drop_mask(nb), rest, v), nb=nb, rest=choose)
    drop = choose(val)
    drop = jnp.concatenate([drop] + [jnp.zeros((BLK_LANES - n_blk, NQB), F32)] * (n_blk < BLK_LANES),
                           axis=0).astype(BF16)
    rhs_scr[...] = jnp.concatenate([jnp.concatenate([drop] * grp, axis=1), qst], axis=0)

    tk = SEL_TILE
    m_scr[...] = jnp.full((1, lanes), NEG_INF, F32)
    acc_scr[...] = jnp.zeros(acc_scr.shape, F32)

    def sel_body(i, carry):
        k0 = pl.multiple_of(i * tk, tk)
        s = _dot(kext_scr[pl.ds(k0, tk), :], rhs_scr[...])
        d = Q_PER_K * qi - (tk // KT) * i
        entries = [jnp.clip(d - c, -Q_PER_K, N_NEAR) + Q_PER_K for c in range(tk // KT)]
        s = s + jnp.concatenate([sb_ref[0, e] for e in entries], axis=0)
        m_prev = m_scr[...]
        m_new = jnp.maximum(m_prev, s.max(axis=0, keepdims=True))
        p = jnp.exp2(s - m_new).astype(BF16)
        vx = jnp.concatenate([vst_ref[0, :, pl.ds(k0, tk)], ones], axis=0)
        acc_scr[...] = acc_scr[...] * jnp.exp2(m_prev - m_new) + _dot(vx, p)
        m_scr[...] = m_new
        return carry

    n_tiles = (KT * Q_PER_K * (qi + 1) + tk - 1) // tk
    lax.fori_loop(0, n_tiles, sel_body, 0)
    acc = acc_scr[...]
    o_s = acc[:HEAD_DIM] * (1.0 / acc[HEAD_DIM:HEAD_DIM + 1])

    gate = jax.nn.sigmoid(gt_ref[0])
    outs = []
    for g in range(grp):
        sl = slice(NQB * g, NQB * (g + 1))
        outs.append(gate[3 * g:3 * g + 1] * o_c[:, sl] + gate[3 * g + 1:3 * g + 2] * o_s[:, sl]
                    + gate[3 * g + 2:3 * g + 3] * o_w[:, sl])
    o_ref[0] = jnp.concatenate(outs, axis=0).T.astype(o_ref.dtype)


def _nsa_attention(qat, gtt, kc, vct, ks, kw, vst, vwt, et, bct, sbt, wbt, ovt):
    bsz, _, t = qat.shape
    hkv = NSA_KV_HEADS
    n_cmp = kc.shape[2]
    gw = NSA_GROUP * HEAD_DIM
    lanes = NSA_GROUP * NQB
    kv_spec = pl.BlockSpec((1, 1, t, HEAD_DIM), lambda hh, bb, i: (bb, hh, 0, 0))
    vt_spec = pl.BlockSpec((1, HEAD_DIM, t), lambda hh, bb, i: (bb, hh, 0))
    tbl = lambda a: pl.BlockSpec((1,) + a.shape[1:], lambda hh, bb, i: (hh, 0, 0, 0))
    const = lambda a: pl.BlockSpec(a.shape, lambda hh, bb, i: (0, 0))
    return pl.pallas_call(
        _nsa_kernel,
        grid=(hkv, bsz, t // NQB),
        in_specs=[
            pl.BlockSpec((1, gw, NQB), lambda hh, bb, i: (bb, hh, i)),
            pl.BlockSpec((1, GATE_ROWS, NQB), lambda hh, bb, i: (bb, hh, i)),
            pl.BlockSpec((1, 1, n_cmp, HEAD_DIM), lambda hh, bb, i: (bb, hh, 0, 0)),
            pl.BlockSpec((1, 1, HEAD_DIM, n_cmp), lambda hh, bb, i: (bb, hh, 0, 0)),
            kv_spec, kv_spec, vt_spec, vt_spec, const(et),
            tbl(bct), tbl(sbt), tbl(wbt), const(ovt),
        ],
        out_specs=pl.BlockSpec((1, NQB, gw), lambda hh, bb, i: (bb, i, hh)),
        out_shape=jax.ShapeDtypeStruct((bsz, t, NSA_HEADS * HEAD_DIM), BF16),
        scratch_shapes=[pltpu.VMEM((t, KEXT_W), BF16),
                        pltpu.VMEM((KEXT_W, lanes), BF16),
                        pltpu.VMEM((1, lanes), F32),
                        pltpu.VMEM((HEAD_DIM + 16, lanes), F32)],
        compiler_params=pltpu.CompilerParams(
            dimension_semantics=("parallel", "parallel", "arbitrary"), vmem_limit_bytes=VMEM_LIMIT),
        name="nsa_attention",
    )(qat, gtt, kc, vct, ks, kw, vst, vwt, et, bct, sbt, wbt, ovt)


def _swa_kernel(sink_ref, qt_ref, kp_ref, kn_ref, vp_ref, vn_ref, wb_ref, o_ref):
    qi = pl.program_id(2)
    qt = qt_ref[0]
    qst = jnp.concatenate([qt[HEAD_DIM * g:HEAD_DIM * (g + 1)] for g in range(SWA_GROUP)], axis=1)
    k = jnp.concatenate([kp_ref[0, 0], kn_ref[0, 0]], axis=0)
    vt = jnp.concatenate([vp_ref[0], vn_ref[0]], axis=1)
    s = _dot(k, qst) + wb_ref[0, jnp.where(qi == 0, 1, 0)]
    sink = sink_ref[0]
    m = jnp.maximum(s.max(axis=0, keepdims=True), sink)
    p = jnp.exp2(s - m).astype(BF16)
    vx = jnp.concatenate([vt, jnp.ones((16, 2 * KT), BF16)], axis=0)
    acc = _dot(vx, p)
    o = acc[:HEAD_DIM] * (1.0 / (acc[HEAD_DIM:HEAD_DIM + 1] + jnp.exp2(sink - m)))
    o = jnp.concatenate([o[:, QB * g:QB * (g + 1)] for g in range(SWA_GROUP)], axis=0)
    o_ref[0] = o.T.astype(o_ref.dtype)


def _swa_attention(sink_rows, qbt, kb, vbt, wbb):
    bsz, _, t = qbt.shape
    gw = SWA_GROUP * HEAD_DIM
    lanes = SWA_GROUP * QB
    kprev = pl.BlockSpec((1, 1, KT, HEAD_DIM), lambda bb, hh, i: (bb, hh, jnp.maximum(i - 1, 0), 0))
    kcur = pl.BlockSpec((1, 1, KT, HEAD_DIM), lambda bb, hh, i: (bb, hh, i, 0))
    vprev = pl.BlockSpec((1, HEAD_DIM, KT), lambda bb, hh, i: (bb, hh, jnp.maximum(i - 1, 0)))
    vcur = pl.BlockSpec((1, HEAD_DIM, KT), lambda bb, hh, i: (bb, hh, i))
    return pl.pallas_call(
        _swa_kernel,
        grid=(bsz, SWA_KV_HEADS, t // QB),
        in_specs=[
            pl.BlockSpec((1, 1, lanes), lambda bb, hh, i: (hh, 0, 0)),
            pl.BlockSpec((1, gw, QB), lambda bb, hh, i: (bb, hh, i)),
            kprev, kcur, vprev, vcur,
            pl.BlockSpec((1,) + wbb.shape[1:], lambda bb, hh, i: (hh, 0, 0, 0)),
        ],
        out_specs=pl.BlockSpec((1, QB, gw), lambda bb, hh, i: (bb, i, hh)),
        out_shape=jax.ShapeDtypeStruct((bsz, t, SWA_HEADS * HEAD_DIM), BF16),
        compiler_params=pltpu.CompilerParams(
            dimension_semantics=("parallel", "parallel", "arbitrary"), vmem_limit_bytes=VMEM_LIMIT),
        name="swa_attention",
    )(sink_rows, qbt, kb, kb, vbt, vbt, wbb)


def _outproj_kernel(x_ref, oa_ref, ob_ref, w_ref, o_ref):
    na = oa_ref.shape[1]
    o_ref[...] = x_ref[...] + _dot(oa_ref[...], w_ref[:na, :]) + _dot(ob_ref[...], w_ref[na:, :])


def _outproj(x, oa, ob, w, layer):
    n, d = x.shape
    tm = min(ROW_TILE, n)
    row = lambda width: pl.BlockSpec((tm, width), lambda i: (i, 0))
    return pl.pallas_call(
        _outproj_kernel,
        grid=(n // tm,),
        in_specs=[row(d), row(oa.shape[1]), row(ob.shape[1]),
                  pl.BlockSpec((None,) + w.shape[1:], lambda i: (layer, 0, 0))],
        out_specs=row(d),
        out_shape=jax.ShapeDtypeStruct((n, d), F32),
        compiler_params=pltpu.CompilerParams(
            dimension_semantics=("parallel",), vmem_limit_bytes=VMEM_LIMIT),
        name="outproj",
    )(x, oa, ob, w)


def _ffn_kernel(x_ref, g_ref, wg_ref, wu_ref, wd_ref, gf_ref, o_ref, h_scr, *, final_norm):
    f = pl.program_id(1)

    @pl.when(f == 0)
    def _():
        x = x_ref[...]
        h_scr[...] = _rms(x, g_ref[...]).astype(BF16)
        o_ref[...] = x

    h = h_scr[...]
    gate = _dot(h, wg_ref[...].astype(BF16))
    act = (gate * jax.nn.sigmoid(gate) * _dot(h, wu_ref[...].astype(BF16))).astype(BF16)
    half = o_ref.shape[1] // 2
    o_ref[:, :half] += _dot(act, wd_ref[:, :half].astype(BF16))
    o_ref[:, half:] += _dot(act, wd_ref[:, half:].astype(BF16))

    if final_norm:
        @pl.when(f == pl.num_programs(1) - 1)
        def _():
            o_ref[...] = _rms(o_ref[...], gf_ref[...])


def _ffn(x, g, wg, wu, wd, g_final, final_norm, layer):
    n, d = x.shape
    dff = wg.shape[2]
    tm = min(FFN_ROW_TILE, n)
    return pl.pallas_call(
        functools.partial(_ffn_kernel, final_norm=final_norm),
        grid=(n // tm, dff // FF_TILE),
        in_specs=[
            pl.BlockSpec((tm, d), lambda i, f: (i, 0)),
            pl.BlockSpec((1, d), lambda i, f: (0, 0)),
            pl.BlockSpec((None, d, FF_TILE), lambda i, f: (layer, 0, f)),
            pl.BlockSpec((None, d, FF_TILE), lambda i, f: (layer, 0, f)),
            pl.BlockSpec((None, FF_TILE, d), lambda i, f: (layer, f, 0)),
            pl.BlockSpec((1, d), lambda i, f: (0, 0)),
        ],
        out_specs=pl.BlockSpec((tm, d), lambda i, f: (i, 0)),
        out_shape=jax.ShapeDtypeStruct((n, d), F32),
        scratch_shapes=[pltpu.VMEM((tm, d), BF16)],
        compiler_params=pltpu.CompilerParams(
            dimension_semantics=("parallel", "arbitrary"), vmem_limit_bytes=VMEM_LIMIT),
        name="ffn",
    )(x, g, wg, wu, wd, g_final)


def _gate_rows(a):
    depth, _, n = a.shape
    a = a.reshape(depth, NSA_KV_HEADS, 3 * NSA_GROUP, n)
    a = jnp.pad(a, ((0, 0), (0, 0), (0, GATE_ROWS - 3 * NSA_GROUP), (0, 0)))
    return a.reshape(depth, N_GATES, n)


def _overlap_t(n_cmp, n_blk):
    ci = jnp.arange(n_cmp)[None, :] * CMP_STRIDE
    blk = jnp.arange(BLK_LANES)[:, None]
    sj = blk * SEL_BLOCK
    ov = (ci < sj + SEL_BLOCK) & (ci + CMP_LEN > sj) & (blk < n_blk)
    return (ov & (jnp.arange(n_cmp)[None, :] < n_cmp - 1)).astype(BF16)


def kernel(x, rel_bias, norm_mix, norm_ffn, w_in, b_in, cmp_pos_k, cmp_pos_v, cmp_k_w1, cmp_k_w2,
           cmp_v_w1, cmp_v_w2, sinks, w_out, w_gate, w_up, w_down, norm_final):
    bsz, t, d = x.shape
    depth = w_in.shape[0]
    assert d == D_MODEL and w_in.shape[2] == COL_VB[1] and rel_bias.shape == (NUM_BUCKETS, NSA_HEADS + SWA_HEADS)
    assert t % SEL_TILE == 0 and t // SEL_BLOCK <= BLK_LANES
    assert (bsz * t) % FFN_ROW_TILE == 0 and w_gate.shape[2] % FF_TILE == 0
    n_cmp = t // CMP_STRIDE
    sel_idx, win_idx, cmp_idx, swa_idx = _bias_indices(n_cmp)
    tbl_a, tbl_b = rel_bias[:, :NSA_HEADS], rel_bias[:, NSA_HEADS:]
    sbt = _expand_tables(tbl_a, sel_idx, NSA_GROUP)
    wbt = _expand_tables(tbl_a, win_idx, NSA_GROUP)
    bct = _expand_tables(tbl_a, cmp_idx, NSA_GROUP)
    wbb = _expand_tables(tbl_b, swa_idx, SWA_GROUP)
    ovt = _overlap_t(n_cmp, t // SEL_BLOCK)
    et = (jnp.arange(t)[:, None] // SEL_BLOCK == jnp.arange(BLK_LANES)[None, :]).astype(BF16)
    w_in_t = jnp.swapaxes(w_in.astype(BF16), -1, -2)
    b_row = jnp.concatenate([b_in[:, a:b] for a, b in ROWMAJOR], axis=-1).astype(F32)
    w_gt, b_gt = _gate_rows(w_in_t[:, COL_GT[0]:COL_GT[1], :]), _gate_rows(b_in[:, COL_GT[0]:COL_GT[1], None])
    w_out16 = w_out.astype(BF16)
    for layer in range(depth):
        qat, kc_raw, vc_raw, ks, kw, vst, vwt, gtt, qbt, kb, vbt = _inproj(
            x, norm_mix[layer][None, :], w_in_t, b_row[layer][None, :],
            b_in[layer].astype(F32)[:, None], w_gt[layer], b_gt[layer].astype(F32), layer)
        kc, vct = _compress(kc_raw, vc_raw, cmp_pos_k[layer], cmp_pos_v[layer],
                            cmp_k_w1[layer].astype(BF16), cmp_k_w2[layer].astype(BF16),
                            cmp_v_w1[layer].astype(BF16), cmp_v_w2[layer].T.astype(BF16))
        o_a = _nsa_attention(qat, gtt, kc, vct, ks, kw, vst, vwt, et, bct, sbt, wbt, ovt)
        sink_rows = jnp.repeat(sinks[layer].astype(F32) * LOG2E, QB).reshape(SWA_KV_HEADS, 1, SWA_GROUP * QB)
        o_b = _swa_attention(sink_rows, qbt, kb, vbt, wbb)
        x2 = _outproj(x.reshape(bsz * t, d), o_a.reshape(bsz * t, -1), o_b.reshape(bsz * t, -1),
                      w_out16, layer)
        x2 = _ffn(x2, norm_ffn[layer][None, :], w_gate, w_up, w_down, norm_final[None, :],
                  layer == depth - 1, layer)
        x = x2.reshape(bsz, t, d)
    return x
```
